```python
import math
import jax, jax.numpy as jnp
from jax import lax
import numpy as np

D_MODEL = 2048
BATCH = 1
SEQ = 8192
DEPTH = 4

CHUNK = 64
Q_BLOCK = 128
NEG_INF = -1e30
EPS = 1e-6

SSM_EXPAND = 2
SSM_D_INNER = SSM_EXPAND * D_MODEL
SSM_HEAD_DIM = 64
SSM_HEADS = SSM_D_INNER // SSM_HEAD_DIM
SSM_GROUPS = 8
SSM_HEADS_PER_GROUP = SSM_HEADS // SSM_GROUPS
SSM_STATE = 128
SSM_CONV = 4
SSM_CONV_DIM = SSM_D_INNER + 2 * SSM_GROUPS * SSM_STATE
SSM_IN_DIM = SSM_D_INNER + SSM_CONV_DIM + SSM_HEADS
DT_MIN = 0.001
DT_MAX = 0.1

ATT_HEADS = 16
ATT_QK_DIM = D_MODEL // ATT_HEADS // 2
ATT_V_DIM = 2 * ATT_QK_DIM
ATT_IN_DIM = 3 * D_MODEL
NUM_BUCKETS = 32
MAX_DISTANCE = 128

MLP_HIDDEN = 4 * D_MODEL

N_SSM_LAYERS = (DEPTH + 1) // 2
N_ATT_LAYERS = DEPTH // 2

kernel_name = "hybrid_ssd_diffattn_trunk"


def rms_norm(x, w, eps=EPS):
    xf = x.astype(jnp.float32)
    xf = xf * lax.rsqrt(jnp.mean(xf * xf, axis=-1, keepdims=True) + eps)
    return xf.astype(x.dtype) * w


def causal_depthwise_conv(x, w, b):
    c = x.shape[-1]
    out = lax.conv_general_dilated(
        x, w[:, None, :].astype(x.dtype), window_strides=(1,),
        padding=[(SSM_CONV - 1, 0)], dimension_numbers=("NWC", "WIO", "NWC"),
        feature_group_count=c)
    return out + b


def ssd_chunked(xs, dt, a, bm, cm):
    bsz, s, _ = xs.shape
    nc = s // CHUNK
    g, r, p, n = SSM_GROUPS, SSM_HEADS_PER_GROUP, SSM_HEAD_DIM, SSM_STATE
    x = xs.astype(jnp.float32).reshape(bsz, nc, CHUNK, g, r, p)
    dtr = dt.reshape(bsz, nc, CHUNK, g, r)
    da = dtr * a.reshape(g, r)
    xdt = x * dtr[..., None]
    bc = bm.astype(jnp.float32).reshape(bsz, nc, CHUNK, g, n)
    cc = cm.astype(jnp.float32).reshape(bsz, nc, CHUNK, g, n)
    a_cs = jnp.cumsum(da, axis=2)

    seg = a_cs[:, :, :, None] - a_cs[:, :, None]
    causal = jnp.tril(jnp.ones((CHUNK, CHUNK), dtype=bool))[:, :, None, None]
    decay = jnp.where(causal, jnp.exp(jnp.where(causal, seg, 0.0)), 0.0)
    cb = jnp.einsum("bclgn,bcsgn->bclsg", cc, bc)
    y_diag = jnp.einsum("bclsgr,bcsgrp->bclgrp", cb[..., None] * decay, xdt)

    decay_states = jnp.exp(a_cs[:, :, -1:] - a_cs)
    states = jnp.einsum("bclgn,bclgr,bclgrp->bcgrpn", bc, decay_states, xdt)
    chunk_decay = jnp.exp(a_cs[:, :, -1])

    def step(h, inp):
        st, dec = inp
        return h * dec[..., None, None] + st, h

    h0 = jnp.zeros((bsz, g, r, p, n), jnp.float32)
    _, prev = lax.scan(step, h0, (jnp.swapaxes(states, 0, 1), jnp.swapaxes(chunk_decay, 0, 1)))
    prev = jnp.swapaxes(prev, 0, 1)

    y_off = jnp.einsum("bclgn,bcgrpn,bclgr->bclgrp", cc, prev, jnp.exp(a_cs))
    return (y_diag + y_off).reshape(bsz, s, SSM_HEADS * p)


def mamba2_mixer(u, in_w, conv_w, conv_b, dt_bias, a_log, d_skip, norm_w, out_w):
    bsz, s, _ = u.shape
    proj = u @ in_w
    z = proj[..., :SSM_D_INNER]
    xbc = proj[..., SSM_D_INNER:SSM_D_INNER + SSM_CONV_DIM]
    dt = proj[..., SSM_D_INNER + SSM_CONV_DIM:]
    xbc = jax.nn.silu(causal_depthwise_conv(xbc, conv_w, conv_b))
    xs = xbc[..., :SSM_D_INNER]
    bm = xbc[..., SSM_D_INNER:SSM_D_INNER + SSM_GROUPS * SSM_STATE]
    cm = xbc[..., SSM_D_INNER + SSM_GROUPS * SSM_STATE:]
    dt = jax.nn.softplus(dt.astype(jnp.float32) + dt_bias.astype(jnp.float32))
    a = -jnp.exp(a_log.astype(jnp.float32))
    y = ssd_chunked(xs, dt, a, bm, cm)
    y = y + (xs.astype(jnp.float32).reshape(bsz, s, SSM_HEADS, SSM_HEAD_DIM)
             * d_skip.astype(jnp.float32)[:, None]).reshape(bsz, s, SSM_D_INNER)
    gated = (y * jax.nn.silu(z.astype(jnp.float32))).reshape(bsz, s, SSM_GROUPS, -1)
    gated = gated * lax.rsqrt(jnp.mean(gated * gated, axis=-1, keepdims=True) + EPS)
    gated = gated.reshape(bsz, s, SSM_D_INNER).astype(u.dtype) * norm_w
    return gated @ out_w


def t5_relative_bucket(rel):
    nb = NUM_BUCKETS // 2
    ret = (rel > 0).astype(jnp.int32) * nb
    n = jnp.abs(rel)
    max_exact = nb // 2
    nf = jnp.maximum(n, max_exact).astype(jnp.float32)
    large = max_exact + (jnp.log(nf / max_exact) / math.log(MAX_DISTANCE / max_exact)
                         * (nb - max_exact)).astype(jnp.int32)
    large = jnp.minimum(large, nb - 1)
    return ret + jnp.where(n < max_exact, n, large)


def diff_attention_mixer(u, in_w, q_norm_w, k_norm_w, lam_q1, lam_k1, lam_q2, lam_k2,
                         subln_w, out_w, rel_bias, lambda_init):
    bsz, s, _ = u.shape
    proj = u @ in_w
    q = proj[..., :D_MODEL].reshape(bsz, s, ATT_HEADS, 2, ATT_QK_DIM)
    k = proj[..., D_MODEL:2 * D_MODEL].reshape(bsz, s, ATT_HEADS, 2, ATT_QK_DIM)
    v = proj[..., 2 * D_MODEL:].reshape(bsz, s, ATT_HEADS, ATT_V_DIM)
    q = rms_norm(q, q_norm_w) * (ATT_QK_DIM ** -0.5)
    k = rms_norm(k, k_norm_w)
    lam = (jnp.exp(jnp.sum(lam_q1.astype(jnp.float32) * lam_k1.astype(jnp.float32)))
           - jnp.exp(jnp.sum(lam_q2.astype(jnp.float32) * lam_k2.astype(jnp.float32)))
           + lambda_init)
    key_pos = jnp.arange(s, dtype=jnp.int32)
    key_chunk = key_pos // CHUNK

    def block(i):
        q0 = i * Q_BLOCK
        qb = lax.dynamic_slice_in_dim(q, q0, Q_BLOCK, axis=1)
        q_pos = q0 + jnp.arange(Q_BLOCK, dtype=jnp.int32)
        logits = jnp.einsum("bqhmd,bkhmd->bhmqk", qb, k).astype(jnp.float32)
        bias = rel_bias[t5_relative_bucket(key_pos[None, :] - q_pos[:, None])]
        logits = logits + jnp.transpose(bias, (2, 0, 1)).astype(jnp.float32)[None, :, None]
        allowed = key_chunk[None, :] <= (q_pos // CHUNK)[:, None]
        logits = jnp.where(allowed, logits, NEG_INF)
        probs = jax.nn.softmax(logits, axis=-1)
        attn = probs[:, :, 0] - lam * probs[:, :, 1]
        return jnp.einsum("bhqk,bkhe->bqhe", attn.astype(v.dtype), v)

    out = lax.map(block, jnp.arange(s // Q_BLOCK))
    out = jnp.transpose(out, (1, 0, 2, 3, 4)).reshape(bsz, s, ATT_HEADS, ATT_V_DIM)
    out = rms_norm(out, subln_w) * (1.0 - lambda_init)
    return out.reshape(bsz, s, ATT_HEADS * ATT_V_DIM) @ out_w


def squared_relu_mlp(u, w1, w2):
    return jnp.square(jax.nn.relu(u @ w1)) @ w2


def setup_inputs(seed: int = 0) -> dict:
    key = jax.random.key(seed)
    ks = jax.random.split(key, 24)
    f32 = jnp.float32

    def nrm(k, shape, scale):
        return jax.random.normal(k, shape, f32) * scale

    dt0 = jnp.exp(jax.random.uniform(ks[5], (N_SSM_LAYERS, SSM_HEADS), f32,
                                     minval=math.log(DT_MIN), maxval=math.log(DT_MAX)))
    return {
        "x": nrm(ks[0], (BATCH, SEQ, D_MODEL), 1.0),
        "norm_mix_w": 1.0 + nrm(ks[1], (DEPTH, D_MODEL), 0.02),
        "norm_mlp_w": 1.0 + nrm(ks[2], (DEPTH, D_MODEL), 0.02),
        "ssm_in_w": nrm(ks[3], (N_SSM_LAYERS, D_MODEL, SSM_IN_DIM), D_MODEL ** -0.5),
        "ssm_conv_w": nrm(ks[4], (N_SSM_LAYERS, SSM_CONV, SSM_CONV_DIM), SSM_CONV ** -0.5),
        "ssm_conv_b": nrm(ks[6], (N_SSM_LAYERS, SSM_CONV_DIM), 0.02),
        "ssm_dt_bias": dt0 + jnp.log(-jnp.expm1(-dt0)),
        "ssm_a_log": jnp.log(jax.random.uniform(ks[7], (N_SSM_LAYERS, SSM_HEADS), f32, minval=1.0, maxval=16.0)),
        "ssm_d": 1.0 + nrm(ks[8], (N_SSM_LAYERS, SSM_HEADS), 0.02),
        "ssm_norm_w": 1.0 + nrm(ks[9], (N_SSM_LAYERS, SSM_D_INNER), 0.02),
        "ssm_out_w": nrm(ks[10], (N_SSM_LAYERS, SSM_D_INNER, D_MODEL), SSM_D_INNER ** -0.5),
        "att_in_w": nrm(ks[11], (N_ATT_LAYERS, D_MODEL, ATT_IN_DIM), D_MODEL ** -0.5),
        "att_q_norm_w": 1.0 + nrm(ks[12], (N_ATT_LAYERS, ATT_QK_DIM), 0.02),
        "att_k_norm_w": 1.0 + nrm(ks[13], (N_ATT_LAYERS, ATT_QK_DIM), 0.02),
        "att_lam_q1": nrm(ks[14], (N_ATT_LAYERS, ATT_QK_DIM), 0.1),
        "att_lam_k1": nrm(ks[15], (N_ATT_LAYERS, ATT_QK_DIM), 0.1),
        "att_lam_q2": nrm(ks[16], (N_ATT_LAYERS, ATT_QK_DIM), 0.1),
        "att_lam_k2": nrm(ks[17], (N_ATT_LAYERS, ATT_QK_DIM), 0.1),
        "att_subln_w": 1.0 + nrm(ks[18], (N_ATT_LAYERS, ATT_V_DIM), 0.02),
        "att_out_w": nrm(ks[19], (N_ATT_LAYERS, ATT_HEADS * ATT_V_DIM, D_MODEL), D_MODEL ** -0.5),
        "rel_bias": nrm(ks[20], (NUM_BUCKETS, ATT_HEADS), 0.5),
        "mlp_w1": nrm(ks[21], (DEPTH, D_MODEL, MLP_HIDDEN), D_MODEL ** -0.5),
        "mlp_w2": nrm(ks[22], (DEPTH, MLP_HIDDEN, D_MODEL), MLP_HIDDEN ** -0.5),
    }


def reference(x, norm_mix_w, norm_mlp_w, ssm_in_w, ssm_conv_w, ssm_conv_b, ssm_dt_bias,
              ssm_a_log, ssm_d, ssm_norm_w, ssm_out_w, att_in_w, att_q_norm_w, att_k_norm_w,
              att_lam_q1, att_lam_k1, att_lam_q2, att_lam_k2, att_subln_w, att_out_w,
              rel_bias, mlp_w1, mlp_w2):
    h = x
    for i in range(DEPTH):
        j = i // 2
        u = rms_norm(h, norm_mix_w[i])
        if i % 2 == 0:
            mix = mamba2_mixer(u, ssm_in_w[j], ssm_conv_w[j], ssm_conv_b[j], ssm_dt_bias[j],
                               ssm_a_log[j], ssm_d[j], ssm_norm_w[j], ssm_out_w[j])
        else:
            lambda_init = 0.8 - 0.6 * math.exp(-0.3 * i)
            mix = diff_attention_mixer(u, att_in_w[j], att_q_norm_w[j], att_k_norm_w[j],
                                       att_lam_q1[j], att_lam_k1[j], att_lam_q2[j], att_lam_k2[j],
                                       att_subln_w[j], att_out_w[j], rel_bias, lambda_init)
        h = h + mix
        u = rms_norm(h, norm_mlp_w[i])
        h = h + squared_relu_mlp(u, mlp_w1[i], mlp_w2[i])
    return h
```

```python
import functools
import math

import jax
import jax.numpy as jnp
from jax import lax
from jax.experimental import pallas as pl
from jax.experimental.pallas import tpu as pltpu

F32 = jnp.float32
BF16 = jnp.bfloat16

EPS = 1e-6
MASK_VALUE = -1e30

SSM_HEAD_DIM = 64
SSM_GROUPS = 8
SSM_STATE = 128
SSM_CONV = 4
ATT_QK_DIM = 64
ATT_V_DIM = 128
ATT_CHUNK = 64
NUM_BUCKETS = 32
MAX_DISTANCE = 128

LANES = 128
SUBLANES = 8
VMEM_LIMIT_BYTES = 56 * 1024 * 1024

ROW_TILE = 1024
COL_TILE = 1024
RES_ROW_TILE = 512
MLP_HIDDEN_TILE = 512
SSD_CHUNK = 256
ATT_TILE = 256
NORM_SLAB = 256


def _params(*sem):
    return pltpu.CompilerParams(dimension_semantics=sem, vmem_limit_bytes=VMEM_LIMIT_BYTES)


def _rms_rows(x, w):
    ms = jnp.mean(x * x, axis=-1, keepdims=True)
    return x * lax.rsqrt(ms + EPS) * w


def _rmsnorm_kernel(x_ref, w_ref, o_ref):
    o_ref[...] = _rms_rows(x_ref[...], w_ref[...]).astype(o_ref.dtype)


def _rmsnorm(x, w, tm):
    s, d = x.shape
    return pl.pallas_call(
        _rmsnorm_kernel,
        out_shape=jax.ShapeDtypeStruct((s, d), BF16),
        grid=(s // tm,),
        in_specs=[pl.BlockSpec((tm, d), lambda i: (i, 0)),
                  pl.BlockSpec((1, d), lambda i: (0, 0))],
        out_specs=pl.BlockSpec((tm, d), lambda i: (i, 0)),
        compiler_params=_params("arbitrary"),
        name="rmsnorm_in",
    )(x, w.reshape(1, d))


def _ssm_in_kernel(u_ref, w_ref, cw_ref, cb_ref, o_ref, carry_ref, *, n_z_tiles):
    n = pl.program_id(0)
    m = pl.program_id(1)
    acc = jnp.dot(u_ref[...], w_ref[...], preferred_element_type=F32)

    @pl.when(n < n_z_tiles)
    def _():
        o_ref[...] = acc.astype(o_ref.dtype)

    @pl.when(n >= n_z_tiles)
    def _():
        @pl.when(m == 0)
        def _():
            carry_ref[...] = jnp.zeros_like(carry_ref)

        prev = carry_ref[...]
        cw = cw_ref[...]
        row8 = lax.broadcasted_iota(jnp.int32, prev.shape, 0)
        out = acc * cw[SSM_CONV - 1:SSM_CONV, :] + cb_ref[...]
        corr = jnp.zeros_like(prev)
        for j in range(1, SSM_CONV):
            wj = cw[SSM_CONV - 1 - j:SSM_CONV - j, :]
            r = pltpu.roll(acc, j, axis=0)
            out = out + r * wj
            rp = pltpu.roll(prev, j, axis=0)
            corr = corr + jnp.where(row8 < j, (rp - r[:SUBLANES]) * wj, 0.0)
        o_ref[...] = (out * jax.nn.sigmoid(out)).astype(o_ref.dtype)
        top = out[:SUBLANES] + corr
        o_ref[:SUBLANES, :] = (top * jax.nn.sigmoid(top)).astype(o_ref.dtype)
        carry_ref[...] = acc[acc.shape[0] - SUBLANES:, :]


def _ssm_in_proj(u, w, conv_w, conv_b, d_inner, tm, tn):
    s, d = u.shape
    n_out = w.shape[1]
    n_z_tiles = d_inner // tn
    kernel = functools.partial(_ssm_in_kernel, n_z_tiles=n_z_tiles)
    conv_idx = lambda n, m: (0, jnp.maximum(n - n_z_tiles, 0))
    return pl.pallas_call(
        kernel,
        out_shape=jax.ShapeDtypeStruct((s, n_out), BF16),
        grid=(n_out // tn, s // tm),
        in_specs=[pl.BlockSpec((tm, d), lambda n, m: (m, 0)),
                  pl.BlockSpec((d, tn), lambda n, m: (0, n)),
                  pl.BlockSpec((SSM_CONV, tn), conv_idx),
                  pl.BlockSpec((1, tn), conv_idx)],
        out_specs=pl.BlockSpec((tm, tn), lambda n, m: (m, n)),
        scratch_shapes=[pltpu.VMEM((SUBLANES, tn), F32)],
        compiler_params=_params("arbitrary", "arbitrary"),
        name="ssm_in_proj",
    )(u, w, conv_w, conv_b)


def _dt_kernel(u_ref, w_ref, b_ref, mult_ref, o_ref, *, chunk, n_heads):
    x = jnp.dot(u_ref[...], w_ref[...], preferred_element_type=F32) + b_ref[...]
    sp = jnp.maximum(x, 0.0) + jnp.log1p(jnp.exp(-jnp.abs(x)))
    v = sp * mult_ref[...]
    ri = lax.broadcasted_iota(jnp.int32, (chunk, chunk), 0)
    ci = lax.broadcasted_iota(jnp.int32, (chunk, chunk), 1)
    tri = (ri >= ci).astype(F32)
    is_da = lax.broadcasted_iota(jnp.int32, (1, v.shape[1]), 1) >= n_heads
    for c in range(v.shape[0] // chunk):
        blk = v[c * chunk:(c + 1) * chunk]
        cs = jnp.dot(tri, blk, precision=lax.Precision.HIGHEST, preferred_element_type=F32)
        o_ref[c * chunk:(c + 1) * chunk, :] = jnp.where(is_da, cs, blk)


def _dt_proj(u, w2, b2, mult, chunk, tm):
    s, d = u.shape
    n = w2.shape[1]
    kernel = functools.partial(_dt_kernel, chunk=chunk, n_heads=n // 2)
    return pl.pallas_call(
        kernel,
        out_shape=jax.ShapeDtypeStruct((s, n), F32),
        grid=(s // tm,),
        in_specs=[pl.BlockSpec((tm, d), lambda i: (i, 0)),
                  pl.BlockSpec((d, n), lambda i: (0, 0)),
                  pl.BlockSpec((1, n), lambda i: (0, 0)),
                  pl.BlockSpec((1, n), lambda i: (0, 0))],
        out_specs=pl.BlockSpec((tm, n), lambda i: (i, 0)),
        compiler_params=_params("arbitrary"),
        name="ssm_dt_proj",
    )(u, w2, b2, mult)


def _ssd_kernel(z_ref, x_ref, b_ref, c_ref, cols_ref, rows_ref, d_ref, nw_ref, o_ref, st_ref,
                *, chunk, heads_per_group):
    @pl.when(pl.program_id(1) == 0)
    def _():
        st_ref[...] = jnp.zeros_like(st_ref)

    r = heads_per_group
    x = x_ref[...]
    bm = b_ref[...]
    cm = c_ref[...]
    bm32 = bm.astype(F32)
    cm32 = cm.astype(F32)
    cols = cols_ref[0]
    rows = rows_ref[0]
    cb = lax.dot_general(cm, bm, (((1,), (1,)), ((), ())), preferred_element_type=F32)
    ri = lax.broadcasted_iota(jnp.int32, (chunk, chunk), 0)
    ci = lax.broadcasted_iota(jnp.int32, (chunk, chunk), 1)
    causal = ri >= ci
    lo = lax.broadcasted_iota(jnp.int32, (1, LANES), 1) < SSM_HEAD_DIM
    zero16 = jnp.zeros((), BF16)

    ys = []
    for p in range(r // 2):
        ms, cs, bw, dec = [], [], [], []
        for jj in range(2):
            j = 2 * p + jj
            a_col = cols[:, r + j:r + j + 1]
            dt_col = cols[:, j:j + 1]
            a_row = rows[r + j:r + j + 1, :]
            dt_row = rows[j:j + 1, :]
            dmat = jnp.exp(jnp.where(causal, a_col - a_row, MASK_VALUE))
            ms.append((cb * dmat * dt_row).astype(BF16))
            cs.append((cm32 * jnp.exp(a_col)).astype(BF16))
            a_last = a_col[chunk - 1:chunk, :]
            bw.append((bm32 * (dt_col * jnp.exp(a_last - a_col))).astype(BF16))
            dec.append(jnp.exp(a_last))
        xp = x[:, p * LANES:(p + 1) * LANES]
        xbd = jnp.concatenate([jnp.where(lo, xp, zero16), jnp.where(lo, zero16, xp)], axis=0)
        st = st_ref[:, p * LANES:(p + 1) * LANES]
        stb = st.astype(BF16)
        stbd = jnp.concatenate([jnp.where(lo, stb, zero16), jnp.where(lo, zero16, stb)], axis=0)
        y = (jnp.dot(jnp.concatenate(ms, axis=1), xbd, preferred_element_type=F32)
             + jnp.dot(jnp.concatenate(cs, axis=1), stbd, preferred_element_type=F32))
        upd = lax.dot_general(jnp.concatenate(bw, axis=0), xbd, (((0,), (0,)), ((), ())),
                              preferred_element_type=F32)
        st_ref[:, p * LANES:(p + 1) * LANES] = st * jnp.where(lo, dec[0], dec[1]) + upd
        ys.append(y)

    y = jnp.concatenate(ys, axis=1) + x.astype(F32) * d_ref[...]
    z = z_ref[...].astype(F32)
    gated = y * (z * jax.nn.sigmoid(z))
    o_ref[...] = _rms_rows(gated, nw_ref[...]).astype(o_ref.dtype)


def _ssd(zx, cols, rows, d_exp, norm_w, d_inner, chunk):
    s = zx.shape[0]
    g = SSM_GROUPS
    gw = d_inner // g
    r = gw // SSM_HEAD_DIM
    n = SSM_STATE
    x_off = d_inner // gw
    b_off = 2 * d_inner // n
    c_off = b_off + g
    kernel = functools.partial(_ssd_kernel, chunk=chunk, heads_per_group=r)
    return pl.pallas_call(
        kernel,
        out_shape=jax.ShapeDtypeStruct((s, d_inner), BF16),
        grid=(g, s // chunk),
        in_specs=[pl.BlockSpec((chunk, gw), lambda gi, c: (c, gi)),
                  pl.BlockSpec((chunk, gw), lambda gi, c: (c, x_off + gi)),
                  pl.BlockSpec((chunk, n), lambda gi, c: (c, b_off + gi)),
                  pl.BlockSpec((chunk, n), lambda gi, c: (c, c_off + gi)),
                  pl.BlockSpec((1, chunk, 2 * r), lambda gi, c: (gi, c, 0)),
                  pl.BlockSpec((1, 2 * r, chunk), lambda gi, c: (gi, 0, c)),
                  pl.BlockSpec((1, gw), lambda gi, c: (0, gi)),
                  pl.BlockSpec((1, gw), lambda gi, c: (0, gi))],
        out_specs=pl.BlockSpec((chunk, gw), lambda gi, c: (c, gi)),
        scratch_shapes=[pltpu.VMEM((n, gw), F32)],
        compiler_params=_params("arbitrary", "arbitrary"),
        name="ssd_scan",
    )(zx, zx, zx, zx, cols, rows, d_exp, norm_w)


def _proj_res_kernel(y_ref, w_ref, h_ref, nw_ref, ho_ref, uo_ref, acc_ref):
    k = pl.program_id(1)

    @pl.when(k == 0)
    def _():
        acc_ref[...] = jnp.zeros_like(acc_ref)

    acc_ref[...] += jnp.dot(y_ref[...], w_ref[...], preferred_element_type=F32)

    @pl.when(k == pl.num_programs(1) - 1)
    def _():
        hn = h_ref[...] + acc_ref[...]
        ho_ref[...] = hn
        uo_ref[...] = _rms_rows(hn, nw_ref[...]).astype(uo_ref.dtype)


def _proj_res(y, w, h, norm_w, tm, tk):
    s, kd = y.shape
    d = w.shape[1]
    return pl.pallas_call(
        _proj_res_kernel,
        out_shape=(jax.ShapeDtypeStruct((s, d), F32), jax.ShapeDtypeStruct((s, d), BF16)),
        grid=(s // tm, kd // tk),
        in_specs=[pl.BlockSpec((tm, tk), lambda m, k: (m, k)),
                  pl.BlockSpec((tk, d), lambda m, k: (k, 0)),
                  pl.BlockSpec((tm, d), lambda m, k: (m, 0)),
                  pl.BlockSpec((1, d), lambda m, k: (0, 0))],
        out_specs=(pl.BlockSpec((tm, d), lambda m, k: (m, 0)),
                   pl.BlockSpec((tm, d), lambda m, k: (m, 0))),
        scratch_shapes=[pltpu.VMEM((tm, d), F32)],
        compiler_params=_params("arbitrary", "arbitrary"),
        name="proj_residual",
    )(y, w, h, norm_w.reshape(1, d))


def _mlp_kernel(u_ref, w1_ref, w2_ref, h_ref, *rest, emit_norm):
    if emit_norm:
        nw_ref, ho_ref, uo_ref, acc_ref = rest
    else:
        ho_ref, acc_ref = rest
    j = pl.program_id(1)

    @pl.when(j == 0)
    def _():
        acc_ref[...] = jnp.zeros_like(acc_ref)

    a = jnp.dot(u_ref[...], w1_ref[...], preferred_element_type=F32)
    a = jnp.square(jnp.maximum(a, 0.0)).astype(BF16)
    acc_ref[...] += jnp.dot(a, w2_ref[...], preferred_element_type=F32)

    @pl.when(j == pl.num_programs(1) - 1)
    def _():
        hn = h_ref[...] + acc_ref[...]
        ho_ref[...] = hn
        if emit_norm:
            uo_ref[...] = _rms_rows(hn, nw_ref[...]).astype(uo_ref.dtype)


def _mlp(u, w1, w2, h, norm_w, tm, th):
    s, d = u.shape
    hd = w1.shape[1]
    emit_norm = norm_w is not None
    row_spec = pl.BlockSpec((tm, d), lambda m, j: (m, 0))
    in_specs = [row_spec,
                pl.BlockSpec((d, th), lambda m, j: (0, j)),
                pl.BlockSpec((th, d), lambda m, j: (j, 0)),
                row_spec]
    args = [u, w1, w2, h]
    out_shape = [jax.ShapeDtypeStruct((s, d), F32)]
    out_specs = [row_spec]
    if emit_norm:
        in_specs.append(pl.BlockSpec((1, d), lambda m, j: (0, 0)))
        args.append(norm_w.reshape(1, d))
        out_shape.append(jax.ShapeDtypeStruct((s, d), BF16))
        out_specs.append(row_spec)
    out = pl.pallas_call(
        functools.partial(_mlp_kernel, emit_norm=emit_norm),
        out_shape=tuple(out_shape),
        grid=(s // tm, hd // th),
        in_specs=in_specs,
        out_specs=tuple(out_specs),
        scratch_shapes=[pltpu.VMEM((tm, d), F32)],
        compiler_params=_params("arbitrary", "arbitrary"),
        name="mlp_residual",
    )(*args)
    return out if emit_norm else (out[0], None)


def _att_in_kernel(u_ref, w_ref, nw_ref, o_ref, *, n_qk_tiles):
    n = pl.program_id(0)
    acc = jnp.dot(u_ref[...], w_ref[...], preferred_element_type=F32)

    @pl.when(n >= n_qk_tiles)
    def _():
        o_ref[...] = acc.astype(o_ref.dtype)

    @pl.when(n < n_qk_tiles)
    def _():
        gi = lax.broadcasted_iota(jnp.int32, (NORM_SLAB, NORM_SLAB), 0) // ATT_QK_DIM
        gj = lax.broadcasted_iota(jnp.int32, (NORM_SLAB, NORM_SLAB), 1) // ATT_QK_DIM
        ones_bd = (gi == gj).astype(BF16)
        nw = nw_ref[...]
        for sl in range(acc.shape[1] // NORM_SLAB):
            a = acc[:, sl * NORM_SLAB:(sl + 1) * NORM_SLAB]
            ss = jnp.dot((a * a).astype(BF16), ones_bd, preferred_element_type=F32)
            out = a * lax.rsqrt(ss * (1.0 / ATT_QK_DIM) + EPS) * nw[:, sl * NORM_SLAB:(sl + 1) * NORM_SLAB]
            o_ref[:, sl * NORM_SLAB:(sl + 1) * NORM_SLAB] = out.astype(o_ref.dtype)


def _att_in_proj(u, w, qk_w, n_qk, tm, tn):
    s, d = u.shape
    n_out = w.shape[1]
    n_qk_tiles = n_qk // tn
    kernel = functools.partial(_att_in_kernel, n_qk_tiles=n_qk_tiles)
    return pl.pallas_call(
        kernel,
        out_shape=jax.ShapeDtypeStruct((s, n_out), BF16),
        grid=(n_out // tn, s // tm),
        in_specs=[pl.BlockSpec((tm, d), lambda n, m: (m, 0)),
                  pl.BlockSpec((d, tn), lambda n, m: (0, n)),
                  pl.BlockSpec((1, tn), lambda n, m: (0, jnp.minimum(n, n_qk_tiles - 1)))],
        out_specs=pl.BlockSpec((tm, tn), lambda n, m: (m, n)),
        compiler_params=_params("arbitrary", "arbitrary"),
        name="att_in_proj",
    )(u, w, qk_w)


def _bias_kernel(rel_ref, far_ref, bk_ref, o_ref):
    h = pl.program_id(0)
    bk = bk_ref[...]
    far = far_ref[h]
    acc = jnp.zeros(bk.shape, F32)
    for b in range(NUM_BUCKETS):
        acc = jnp.where(bk == b, rel_ref[b, h] - far, acc)
    o_ref[0] = jnp.where(bk < 0, MASK_VALUE, acc)


def _bias_tiles(rel_bias, far_bias, buckets):
    nb, nh = rel_bias.shape
    _, t, _ = buckets.shape
    return pl.pallas_call(
        _bias_kernel,
        out_shape=jax.ShapeDtypeStruct((nh, 2, t, t), F32),
        grid=(nh,),
        in_specs=[pl.BlockSpec(memory_space=pltpu.SMEM),
                  pl.BlockSpec(memory_space=pltpu.SMEM),
                  pl.BlockSpec((2, t, t), lambda h: (0, 0, 0))],
        out_specs=pl.BlockSpec((1, 2, t, t), lambda h: (h, 0, 0, 0)),
        compiler_params=_params("arbitrary"),
        name="att_bias_tiles",
    )(rel_bias, far_bias, buckets)


def _t5_bucket(rel):
    nb = NUM_BUCKETS // 2
    ret = (rel > 0).astype(jnp.int32) * nb
    n = jnp.abs(rel)
    max_exact = nb // 2
    nf = jnp.maximum(n, max_exact).astype(F32)
    large = max_exact + (jnp.log(nf / max_exact) / math.log(MAX_DISTANCE / max_exact)
                         * (nb - max_exact)).astype(jnp.int32)
    large = jnp.minimum(large, nb - 1)
    return ret + jnp.where(n < max_exact, n, large)


def _attn_kernel(lam_ref, q_ref, k_ref, v_ref, bias_ref, sw_ref, o_ref, m_ref, l_ref, acc_ref,
                 *, tile, out_scale):
    qi = pl.program_id(1)
    t = tile
    lo = lax.broadcasted_iota(jnp.int32, (1, LANES), 1) < ATT_QK_DIM
    zero16 = jnp.zeros((), BF16)
    q = q_ref[...]
    qq = jnp.concatenate([jnp.where(lo, q, zero16), jnp.where(lo, zero16, q)], axis=0)

    m_ref[...] = jnp.full_like(m_ref, MASK_VALUE)
    l_ref[...] = jnp.zeros_like(l_ref)
    acc_ref[...] = jnp.zeros_like(acc_ref)

    def step(kt, bias):
        start = pl.multiple_of(kt * t, t)
        k = k_ref[pl.ds(start, t), :]
        v = v_ref[pl.ds(start, t), :]
        s = lax.dot_general(qq, k, (((1,), (1,)), ((), ())), preferred_element_type=F32)
        if bias is not None:
            s = s + jnp.concatenate([bias, bias], axis=0)
        m_old = m_ref[...]
        m_new = jnp.maximum(m_old, jnp.max(s, axis=-1, keepdims=True))
        alpha = jnp.exp(m_old - m_new)
        p = jnp.exp(s - m_new)
        l_ref[...] = alpha * l_ref[...] + jnp.sum(p, axis=-1, keepdims=True)
        acc_ref[...] = alpha * acc_ref[...] + jnp.dot(p.astype(BF16), v, preferred_element_type=F32)
        m_ref[...] = m_new

    def far_body(kt, carry):
        step(kt, None)
        return carry

    lax.fori_loop(0, jnp.maximum(qi - 1, 0), far_body, 0)

    @pl.when(qi > 0)
    def _():
        step(qi - 1, bias_ref[0, 1])

    step(qi, bias_ref[0, 0])

    o = acc_ref[...] / l_ref[...]
    o = o[:t] - lam_ref[0] * o[t:]
    o_ref[...] = (_rms_rows(o, sw_ref[...]) * out_scale).astype(o_ref.dtype)


def _attention(lam, qkv, bias, subln_w, n_heads, tile, out_scale):
    s = qkv.shape[0]
    kernel = functools.partial(_attn_kernel, tile=tile, out_scale=out_scale)
    return pl.pallas_call(
        kernel,
        out_shape=jax.ShapeDtypeStruct((s, n_heads * ATT_V_DIM), BF16),
        grid=(n_heads, s // tile),
        in_specs=[pl.BlockSpec(memory_space=pltpu.SMEM),
                  pl.BlockSpec((tile, LANES), lambda h, i: (i, h)),
                  pl.BlockSpec((s, LANES), lambda h, i: (0, n_heads + h)),
                  pl.BlockSpec((s, LANES), lambda h, i: (0, 2 * n_heads + h)),
                  pl.BlockSpec((1, 2, tile, tile), lambda h, i: (h, 0, 0, 0)),
                  pl.BlockSpec((1, ATT_V_DIM), lambda h, i: (0, 0))],
        out_specs=pl.BlockSpec((tile, ATT_V_DIM), lambda h, i: (i, h)),
        scratch_shapes=[pltpu.VMEM((2 * tile, 1), F32),
                        pltpu.VMEM((2 * tile, 1), F32),
                        pltpu.VMEM((2 * tile, ATT_V_DIM), F32)],
        compiler_params=_params("arbitrary", "arbitrary"),
        name="diff_attention",
    )(lam, qkv, qkv, qkv, bias, subln_w.reshape(1, ATT_V_DIM))


def kernel(x, norm_mix_w, norm_mlp_w, ssm_in_w, ssm_conv_w, ssm_conv_b, ssm_dt_bias, ssm_a_log, ssm_d, ssm_norm_w, ssm_out_w, att_in_w, att_q_norm_w, att_k_norm_w, att_lam_q1, att_lam_k1, att_lam_q2, att_lam_k2, att_subln_w, att_out_w, rel_bias, mlp_w1, mlp_w2):
    bsz, s, d = x.shape
    assert bsz == 1
    depth = norm_mix_w.shape[0]
    d_inner = ssm_out_w.shape[1]
    n_ssm_heads = ssm_dt_bias.shape[1]
    conv_dim = ssm_conv_w.shape[2]
    n_att_heads = rel_bias.shape[1]
    r = n_ssm_heads // SSM_GROUPS

    tm = min(ROW_TILE, s)
    tmr = min(RES_ROW_TILE, s)
    chunk = min(SSD_CHUNK, s)
    tile = min(ATT_TILE, s)
    assert s % tm == 0 and s % tmr == 0 and s % chunk == 0 and s % tile == 0
    assert tile % ATT_CHUNK == 0 and tile > 90

    h = x.reshape(s, d)
    u = _rmsnorm(h, norm_mix_w[0], tmr)

    ti = jnp.arange(tile, dtype=jnp.int32)[:, None]
    tj = jnp.arange(tile, dtype=jnp.int32)[None, :]
    buckets = _t5_bucket(jnp.stack([tj - ti, tj - ti - tile]))
    allowed = (tj // ATT_CHUNK) <= (ti // ATT_CHUNK)
    buckets = buckets.at[0].set(jnp.where(allowed, buckets[0], -1))
    far_bias = rel_bias[_t5_bucket(jnp.int32(-2 * tile))]
    bias = _bias_tiles(rel_bias, far_bias, buckets)

    for i in range(depth):
        j = i // 2
        if i % 2 == 0:
            w_in = ssm_in_w[j]
            w_zx = w_in[:, :d_inner + conv_dim].astype(BF16)
            w_dt = w_in[:, d_inner + conv_dim:].astype(BF16)
            zx = _ssm_in_proj(u, w_zx, ssm_conv_w[j], ssm_conv_b[j].reshape(1, conv_dim), d_inner, tm, COL_TILE)
            a = -jnp.exp(ssm_a_log[j].astype(F32))
            mult = jnp.concatenate([jnp.ones_like(a), a]).reshape(1, -1)
            dt_b = jnp.tile(ssm_dt_bias[j].astype(F32), 2).reshape(1, -1)
            nar = _dt_proj(u, jnp.concatenate([w_dt, w_dt], axis=1), dt_b, mult, chunk, tm)
            cols = nar.reshape(s, 2, SSM_GROUPS, r).transpose(2, 0, 1, 3).reshape(SSM_GROUPS, s, 2 * r)
            rows = cols.transpose(0, 2, 1)
            d_exp = jnp.repeat(ssm_d[j].astype(F32), SSM_HEAD_DIM).reshape(1, d_inner)
            y = _ssd(zx, cols, rows, d_exp, ssm_norm_w[j].reshape(1, d_inner), d_inner, chunk)
            h, u = _proj_res(y, ssm_out_w[j].astype(BF16), h, norm_mlp_w[i], tmr, d_inner // 2)
        else:
            lambda_init = 0.8 - 0.6 * math.exp(-0.3 * i)
            lam = (jnp.exp(jnp.sum(att_lam_q1[j].astype(F32) * att_lam_k1[j].astype(F32)))
                   - jnp.exp(jnp.sum(att_lam_q2[j].astype(F32) * att_lam_k2[j].astype(F32)))
                   + lambda_init).reshape(1)
            reps = d // ATT_QK_DIM
            qk_w = jnp.concatenate([jnp.tile(att_q_norm_w[j], reps) * (ATT_QK_DIM ** -0.5),
                                    jnp.tile(att_k_norm_w[j], reps)]).reshape(1, 2 * d)
            qkv = _att_in_proj(u, att_in_w[j].astype(BF16), qk_w, 2 * d, tm, COL_TILE)
            o = _attention(lam, qkv, bias, att_subln_w[j], n_att_heads, tile, 1.0 - lambda_init)
            h, u = _proj_res(o, att_out_w[j].astype(BF16), h, norm_mlp_w[i], tmr, d)
        nw_next = norm_mix_w[i + 1] if i + 1 < depth else None
        h, u = _mlp(u, mlp_w1[i].astype(BF16), mlp_w2[i].astype(BF16), h, nw_next, tmr, MLP_HIDDEN_TILE)
    return h.reshape(bsz, s, d)
```

```python
import functools
import math

import jax
import jax.numpy as jnp
from jax import lax
from jax.experimental import pallas as pl
from jax.experimental.pallas import tpu as pltpu

F32 = jnp.float32
BF16 = jnp.bfloat16

EPS = 1e-6
MASK_VALUE = -1e30

SSM_HEAD_DIM = 64
SSM_GROUPS = 8
SSM_STATE = 128
SSM_CONV = 4
ATT_QK_DIM = 64
ATT_V_DIM = 128
ATT_CHUNK = 64
NUM_BUCKETS = 32
MAX_DISTANCE = 128

LANES = 128
SUBLANES = 8
VMEM_LIMIT_BYTES = 56 * 1024 * 1024

ROW_TILE = 1024
COL_TILE = 1024
RES_ROW_TILE = 512
MLP_HIDDEN_TILE = 512
SSD_CHUNK = 256
ATT_TILE = 256
ATT_FAR_WIDTHS = (4, 2, 1)

LOG2E = 1.4426950408889634
EXP2_SAFE_RANGE = 120.0
NORM_SLAB = 256


def _params(*sem):
    return pltpu.CompilerParams(dimension_semantics=sem, vmem_limit_bytes=VMEM_LIMIT_BYTES)


def _rms_rows(x, w):
    ms = jnp.mean(x * x, axis=-1, keepdims=True)
    return x * lax.rsqrt(ms + EPS) * w


def _rmsnorm_kernel(x_ref, w_ref, o_ref):
    o_ref[...] = _rms_rows(x_ref[...], w_ref[...]).astype(o_ref.dtype)


def _rmsnorm(x, w, tm):
    s, d = x.shape
    return pl.pallas_call(
        _rmsnorm_kernel,
        out_shape=jax.ShapeDtypeStruct((s, d), BF16),
        grid=(s // tm,),
        in_specs=[pl.BlockSpec((tm, d), lambda i: (i, 0)),
                  pl.BlockSpec((1, d), lambda i: (0, 0))],
        out_specs=pl.BlockSpec((tm, d), lambda i: (i, 0)),
        compiler_params=_params("arbitrary"),
        name="rmsnorm_in",
    )(x, w.reshape(1, d))


def _ssm_in_kernel(u_ref, w_ref, cw_ref, cb_ref, o_ref, carry_ref, *, n_z_tiles):
    n = pl.program_id(0)
    m = pl.program_id(1)
    acc = jnp.dot(u_ref[...], w_ref[...], preferred_element_type=F32)

    @pl.when(n < n_z_tiles)
    def _():
        o_ref[...] = acc.astype(o_ref.dtype)

    @pl.when(n >= n_z_tiles)
    def _():
        @pl.when(m == 0)
        def _():
            carry_ref[...] = jnp.zeros_like(carry_ref)

        prev = carry_ref[...]
        cw = cw_ref[...]
        row8 = lax.broadcasted_iota(jnp.int32, prev.shape, 0)
        out = acc * cw[SSM_CONV - 1:SSM_CONV, :] + cb_ref[...]
        corr = jnp.zeros_like(prev)
        for j in range(1, SSM_CONV):
            wj = cw[SSM_CONV - 1 - j:SSM_CONV - j, :]
            r = pltpu.roll(acc, j, axis=0)
            out = out + r * wj
            rp = pltpu.roll(prev, j, axis=0)
            corr = corr + jnp.where(row8 < j, (rp - r[:SUBLANES]) * wj, 0.0)
        o_ref[...] = (out * jax.nn.sigmoid(out)).astype(o_ref.dtype)
        top = out[:SUBLANES] + corr
        o_ref[:SUBLANES, :] = (top * jax.nn.sigmoid(top)).astype(o_ref.dtype)
        carry_ref[...] = acc[acc.shape[0] - SUBLANES:, :]


def _ssm_in_proj(u, w, conv_w, conv_b, d_inner, tm, tn):
    s, d = u.shape
    n_out = w.shape[1]
    n_z_tiles = d_inner // tn
    kernel = functools.partial(_ssm_in_kernel, n_z_tiles=n_z_tiles)
    conv_idx = lambda n, m: (0, jnp.maximum(n - n_z_tiles, 0))
    return pl.pallas_call(
        kernel,
        out_shape=jax.ShapeDtypeStruct((s, n_out), BF16),
        grid=(n_out // tn, s // tm),
        in_specs=[pl.BlockSpec((tm, d), lambda n, m: (m, 0)),
                  pl.BlockSpec((d, tn), lambda n, m: (0, n)),
                  pl.BlockSpec((SSM_CONV, tn), conv_idx),
                  pl.BlockSpec((1, tn), conv_idx)],
        out_specs=pl.BlockSpec((tm, tn), lambda n, m: (m, n)),
        scratch_shapes=[pltpu.VMEM((SUBLANES, tn), F32)],
        compiler_params=_params("arbitrary", "arbitrary"),
        name="ssm_in_proj",
    )(u, w, conv_w, conv_b)


def _dt_kernel(u_ref, w_ref, b_ref, mult_ref, o_ref, *, chunk, n_heads):
    x = jnp.dot(u_ref[...], w_ref[...], preferred_element_type=F32) + b_ref[...]
    sp = jnp.maximum(x, 0.0) + jnp.log1p(jnp.exp(-jnp.abs(x)))
    v = sp * mult_ref[...]
    ri = lax.broadcasted_iota(jnp.int32, (chunk, chunk), 0)
    ci = lax.broadcasted_iota(jnp.int32, (chunk, chunk), 1)
    tri = (ri >= ci).astype(F32)
    is_da = lax.broadcasted_iota(jnp.int32, (1, v.shape[1]), 1) >= n_heads
    for c in range(v.shape[0] // chunk):
        blk = v[c * chunk:(c + 1) * chunk]
        cs = jnp.dot(tri, blk, precision=lax.Precision.HIGHEST, preferred_element_type=F32)
        o_ref[c * chunk:(c + 1) * chunk, :] = jnp.where(is_da, cs, blk)


def _dt_proj(u, w2, b2, mult, chunk, tm):
    s, d = u.shape
    n = w2.shape[1]
    kernel = functools.partial(_dt_kernel, chunk=chunk, n_heads=n // 2)
    return pl.pallas_call(
        kernel,
        out_shape=jax.ShapeDtypeStruct((s, n), F32),
        grid=(s // tm,),
        in_specs=[pl.BlockSpec((tm, d), lambda i: (i, 0)),
                  pl.BlockSpec((d, n), lambda i: (0, 0)),
                  pl.BlockSpec((1, n), lambda i: (0, 0)),
                  pl.BlockSpec((1, n), lambda i: (0, 0))],
        out_specs=pl.BlockSpec((tm, n), lambda i: (i, 0)),
        compiler_params=_params("arbitrary"),
        name="ssm_dt_proj",
    )(u, w2, b2, mult)


def _ssd_kernel(z_ref, x_ref, b_ref, c_ref, cols_ref, rows_ref, d_ref, nw_ref, o_ref, st_ref,
                *, chunk, heads_per_group):
    @pl.when(pl.program_id(1) == 0)
    def _():
        st_ref[...] = jnp.zeros_like(st_ref)

    r = heads_per_group
    x = x_ref[...]
    bm = b_ref[...]
    cm = c_ref[...]
    bm32 = bm.astype(F32)
    cm32 = cm.astype(F32)
    cols = cols_ref[0]
    rows = rows_ref[0]
    cb = lax.dot_general(cm, bm, (((1,), (1,)), ((), ())), preferred_element_type=F32)
    ri = lax.broadcasted_iota(jnp.int32, (chunk, chunk), 0)
    ci = lax.broadcasted_iota(jnp.int32, (chunk, chunk), 1)
    causal = ri >= ci
    lo = lax.broadcasted_iota(jnp.int32, (1, LANES), 1) < SSM_HEAD_DIM
    zero16 = jnp.zeros((), BF16)

    ys = []
    for p in range(r // 2):
        ms, cs, bw, dec = [], [], [], []
        for jj in range(2):
            j = 2 * p + jj
            a_col = cols[:, r + j:r + j + 1]
            dt_col = cols[:, j:j + 1]
            a_row = rows[r + j:r + j + 1, :]
            dt_row = rows[j:j + 1, :]
            dmat = jnp.exp(jnp.where(causal, a_col - a_row, MASK_VALUE))
            ms.append((cb * dmat * dt_row).astype(BF16))
            cs.append((cm32 * jnp.exp(a_col)).astype(BF16))
            a_last = a_col[chunk - 1:chunk, :]
            bw.append((bm32 * (dt_col * jnp.exp(a_last - a_col))).astype(BF16))
            dec.append(jnp.exp(a_last))
        xp = x[:, p * LANES:(p + 1) * LANES]
        xbd = jnp.concatenate([jnp.where(lo, xp, zero16), jnp.where(lo, zero16, xp)], axis=0)
        st = st_ref[:, p * LANES:(p + 1) * LANES]
        stb = st.astype(BF16)
        stbd = jnp.concatenate([jnp.where(lo, stb, zero16), jnp.where(lo, zero16, stb)], axis=0)
        y = (jnp.dot(jnp.concatenate(ms, axis=1), xbd, preferred_element_type=F32)
             + jnp.dot(jnp.concatenate(cs, axis=1), stbd, preferred_element_type=F32))
        upd = lax.dot_general(jnp.concatenate(bw, axis=0), xbd, (((0,), (0,)), ((), ())),
                              preferred_element_type=F32)
        st_ref[:, p * LANES:(p + 1) * LANES] = st * jnp.where(lo, dec[0], dec[1]) + upd
        ys.append(y)

    y = jnp.concatenate(ys, axis=1) + x.astype(F32) * d_ref[...]
    z = z_ref[...].astype(F32)
    gated = y * (z * jax.nn.sigmoid(z))
    o_ref[...] = _rms_rows(gated, nw_ref[...]).astype(o_ref.dtype)


def _ssd(zx, cols, rows, d_exp, norm_w, d_inner, chunk):
    s = zx.shape[0]
    g = SSM_GROUPS
    gw = d_inner // g
    r = gw // SSM_HEAD_DIM
    n = SSM_STATE
    x_off = d_inner // gw
    b_off = 2 * d_inner // n
    c_off = b_off + g
    kernel = functools.partial(_ssd_kernel, chunk=chunk, heads_per_group=r)
    return pl.pallas_call(
        kernel,
        out_shape=jax.ShapeDtypeStruct((s, d_inner), BF16),
        grid=(g, s // chunk),
        in_specs=[pl.BlockSpec((chunk, gw), lambda gi, c: (c, gi)),
                  pl.BlockSpec((chunk, gw), lambda gi, c: (c, x_off + gi)),
                  pl.BlockSpec((chunk, n), lambda gi, c: (c, b_off + gi)),
                  pl.BlockSpec((chunk, n), lambda gi, c: (c, c_off + gi)),
                  pl.BlockSpec((1, chunk, 2 * r), lambda gi, c: (gi, c, 0)),
                  pl.BlockSpec((1, 2 * r, chunk), lambda gi, c: (gi, 0, c)),
                  pl.BlockSpec((1, gw), lambda gi, c: (0, gi)),
                  pl.BlockSpec((1, gw), lambda gi, c: (0, gi))],
        out_specs=pl.BlockSpec((chunk, gw), lambda gi, c: (c, gi)),
        scratch_shapes=[pltpu.VMEM((n, gw), F32)],
        compiler_params=_params("arbitrary", "arbitrary"),
        name="ssd_scan",
    )(zx, zx, zx, zx, cols, rows, d_exp, norm_w)


def _proj_res_kernel(y_ref, w_ref, h_ref, nw_ref, ho_ref, uo_ref, acc_ref):
    k = pl.program_id(1)

    @pl.when(k == 0)
    def _():
        acc_ref[...] = jnp.zeros_like(acc_ref)

    acc_ref[...] += jnp.dot(y_ref[...], w_ref[...], preferred_element_type=F32)

    @pl.when(k == pl.num_programs(1) - 1)
    def _():
        hn = h_ref[...] + acc_ref[...]
        ho_ref[...] = hn
        uo_ref[...] = _rms_rows(hn, nw_ref[...]).astype(uo_ref.dtype)


def _proj_res(y, w, h, norm_w, tm, tk):
    s, kd = y.shape
    d = w.shape[1]
    return pl.pallas_call(
        _proj_res_kernel,
        out_shape=(jax.ShapeDtypeStruct((s, d), F32), jax.ShapeDtypeStruct((s, d), BF16)),
        grid=(s // tm, kd // tk),
        in_specs=[pl.BlockSpec((tm, tk), lambda m, k: (m, k)),
                  pl.BlockSpec((tk, d), lambda m, k: (k, 0)),
                  pl.BlockSpec((tm, d), lambda m, k: (m, 0)),
                  pl.BlockSpec((1, d), lambda m, k: (0, 0))],
        out_specs=(pl.BlockSpec((tm, d), lambda m, k: (m, 0)),
                   pl.BlockSpec((tm, d), lambda m, k: (m, 0))),
        scratch_shapes=[pltpu.VMEM((tm, d), F32)],
        compiler_params=_params("arbitrary", "arbitrary"),
        name="proj_residual",
    )(y, w, h, norm_w.reshape(1, d))


def _mlp_kernel(u_ref, w1_ref, w2_ref, h_ref, *rest, emit_norm):
    if emit_norm:
        nw_ref, ho_ref, uo_ref, acc_ref = rest
    else:
        ho_ref, acc_ref = rest
    j = pl.program_id(1)

    @pl.when(j == 0)
    def _():
        acc_ref[...] = jnp.zeros_like(acc_ref)

    a = jnp.dot(u_ref[...], w1_ref[...], preferred_element_type=F32)
    a = jnp.square(jnp.maximum(a, 0.0)).astype(BF16)
    acc_ref[...] += jnp.dot(a, w2_ref[...], preferred_element_type=F32)

    @pl.when(j == pl.num_programs(1) - 1)
    def _():
        hn = h_ref[...] + acc_ref[...]
        ho_ref[...] = hn
        if emit_norm:
            uo_ref[...] = _rms_rows(hn, nw_ref[...]).astype(uo_ref.dtype)


def _mlp(u, w1, w2, h, norm_w, tm, th):
    s, d = u.shape
    hd = w1.shape[1]
    emit_norm = norm_w is not None
    row_spec = pl.BlockSpec((tm, d), lambda m, j: (m, 0))
    in_specs = [row_spec,
                pl.BlockSpec((d, th), lambda m, j: (0, j)),
                pl.BlockSpec((th, d), lambda m, j: (j, 0)),
                row_spec]
    args = [u, w1, w2, h]
    out_shape = [jax.ShapeDtypeStruct((s, d), F32)]
    out_specs = [row_spec]
    if emit_norm:
        in_specs.append(pl.BlockSpec((1, d), lambda m, j: (0, 0)))
        args.append(norm_w.reshape(1, d))
        out_shape.append(jax.ShapeDtypeStruct((s, d), BF16))
        out_specs.append(row_spec)
    out = pl.pallas_call(
        functools.partial(_mlp_kernel, emit_norm=emit_norm),
        out_shape=tuple(out_shape),
        grid=(s // tm, hd // th),
        in_specs=in_specs,
        out_specs=tuple(out_specs),
        scratch_shapes=[pltpu.VMEM((tm, d), F32)],
        compiler_params=_params("arbitrary", "arbitrary"),
        name="mlp_residual",
    )(*args)
    return out if emit_norm else (out[0], None)


def _att_in_kernel(u_ref, w_ref, nw_ref, o_ref, *, n_qk_tiles):
    n = pl.program_id(0)
    acc = jnp.dot(u_ref[...], w_ref[...], preferred_element_type=F32)

    @pl.when(n >= n_qk_tiles)
    def _():
        o_ref[...] = acc.astype(o_ref.dtype)

    @pl.when(n < n_qk_tiles)
    def _():
        gi = lax.broadcasted_iota(jnp.int32, (NORM_SLAB, NORM_SLAB), 0) // ATT_QK_DIM
        gj = lax.broadcasted_iota(jnp.int32, (NORM_SLAB, NORM_SLAB), 1) // ATT_QK_DIM
        ones_bd = (gi == gj).astype(BF16)
        nw = nw_ref[...]
        for sl in range(acc.shape[1] // NORM_SLAB):
            a = acc[:, sl * NORM_SLAB:(sl + 1) * NORM_SLAB]
            ss = jnp.dot((a * a).astype(BF16), ones_bd, preferred_element_type=F32)
            out = a * lax.rsqrt(ss * (1.0 / ATT_QK_DIM) + EPS) * nw[:, sl * NORM_SLAB:(sl + 1) * NORM_SLAB]
            o_ref[:, sl * NORM_SLAB:(sl + 1) * NORM_SLAB] = out.astype(o_ref.dtype)


def _att_in_proj(u, w, qk_w, n_qk, tm, tn):
    s, d = u.shape
    n_out = w.shape[1]
    n_qk_tiles = n_qk // tn
    kernel = functools.partial(_att_in_kernel, n_qk_tiles=n_qk_tiles)
    return pl.pallas_call(
        kernel,
        out_shape=jax.ShapeDtypeStruct((s, n_out), BF16),
        grid=(n_out // tn, s // tm),
        in_specs=[pl.BlockSpec((tm, d), lambda n, m: (m, 0)),
                  pl.BlockSpec((d, tn), lambda n, m: (0, n)),
                  pl.BlockSpec((1, tn), lambda n, m: (0, jnp.minimum(n, n_qk_tiles - 1)))],
        out_specs=pl.BlockSpec((tm, tn), lambda n, m: (m, n)),
        compiler_params=_params("arbitrary", "arbitrary"),
        name="att_in_proj",
    )(u, w, qk_w)


def _bias_kernel(rel_ref, far_ref, bk_ref, o_ref):
    h = pl.program_id(0)
    bk = bk_ref[...]
    far = far_ref[h]
    acc = jnp.zeros(bk.shape, F32)
    for b in range(NUM_BUCKETS):
        acc = jnp.where(bk == b, (rel_ref[b, h] - far) * LOG2E, acc)
    o_ref[0] = jnp.where(bk < 0, MASK_VALUE, acc)


def _bias_tiles(rel_bias, far_bias, buckets):
    nb, nh = rel_bias.shape
    t, t2 = buckets.shape
    return pl.pallas_call(
        _bias_kernel,
        out_shape=jax.ShapeDtypeStruct((nh, t, t2), F32),
        grid=(nh,),
        in_specs=[pl.BlockSpec(memory_space=pltpu.SMEM),
                  pl.BlockSpec(memory_space=pltpu.SMEM),
                  pl.BlockSpec((t, t2), lambda h: (0, 0))],
        out_specs=pl.BlockSpec((1, t, t2), lambda h: (h, 0, 0)),
        compiler_params=_params("arbitrary"),
        name="att_bias_tiles",
    )(rel_bias, far_bias, buckets)


def _t5_bucket(rel):
    nb = NUM_BUCKETS // 2
    ret = (rel > 0).astype(jnp.int32) * nb
    n = jnp.abs(rel)
    max_exact = nb // 2
    nf = jnp.maximum(n, max_exact).astype(F32)
    large = max_exact + (jnp.log(nf / max_exact) / math.log(MAX_DISTANCE / max_exact)
                         * (nb - max_exact)).astype(jnp.int32)
    large = jnp.minimum(large, nb - 1)
    return ret + jnp.where(n < max_exact, n, large)


def _stack_maps(q):
    lo = lax.broadcasted_iota(jnp.int32, (1, LANES), 1) < ATT_QK_DIM
    zero16 = jnp.zeros((), BF16)
    return jnp.concatenate([jnp.where(lo, q, zero16), jnp.where(lo, zero16, q)], axis=0)


def _attn_finish(acc, l, lam, sw, out_scale, t, o_ref):
    o = acc / l
    o = o[:t] - lam * o[t:]
    o_ref[...] = (_rms_rows(o, sw) * out_scale).astype(o_ref.dtype)


def _attn_fixed_kernel(lam_ref, shift_ref, q_ref, k_ref, v_ref, bias_ref, sw_ref, o_ref, l_ref, acc_ref,
                       *, tile, far_widths, out_scale):
    shift = shift_ref[pl.program_id(0)]
    qi = pl.program_id(1)
    t = tile
    qq = _stack_maps(q_ref[...])
    l_ref[...] = jnp.zeros_like(l_ref)
    acc_ref[...] = jnp.zeros_like(acc_ref)

    def step(first_tile, n_tiles, bias):
        width = n_tiles * t
        start = pl.multiple_of(first_tile * t, t)
        k = k_ref[pl.ds(start, width), :]
        v = v_ref[pl.ds(start, width), :]
        s = lax.dot_general(qq, k, (((1,), (1,)), ((), ())), preferred_element_type=F32)
        if bias is None:
            p = jnp.exp2(s - shift)
        else:
            b = bias - shift
            p = jnp.exp2(s + jnp.concatenate([b, b], axis=0))
        part = p[:, :LANES]
        for c in range(1, width // LANES):
            part = part + p[:, c * LANES:(c + 1) * LANES]
        l_ref[...] += part
        acc_ref[...] += jnp.dot(p.astype(BF16), v, preferred_element_type=F32)

    n_far = jnp.maximum(qi - 1, 0)
    w0 = far_widths[0]

    def wide_body(i, carry):
        step(i * w0, w0, None)
        return carry

    lax.fori_loop(0, n_far // w0, wide_body, 0)
    for w in far_widths[1:]:
        @pl.when((n_far & w) != 0)
        def _(w=w):
            step(n_far & ~(2 * w - 1), w, None)

    @pl.when(qi > 0)
    def _():
        step(qi - 1, 2, bias_ref[0])

    @pl.when(qi == 0)
    def _():
        step(0, 1, bias_ref[0, :, t:])

    l = jnp.sum(l_ref[...], axis=-1, keepdims=True)
    _attn_finish(acc_ref[...], l, lam_ref[0], sw_ref[...], out_scale, t, o_ref)


def _attn_online_kernel(lam_ref, q_ref, k_ref, v_ref, bias_ref, sw_ref, o_ref, m_ref, l_ref, acc_ref,
                        *, tile, out_scale):
    qi = pl.program_id(1)
    t = tile
    qq = _stack_maps(q_ref[...])
    m_ref[...] = jnp.full_like(m_ref, MASK_VALUE)
    l_ref[...] = jnp.zeros_like(l_ref)
    acc_ref[...] = jnp.zeros_like(acc_ref)

    def step(kt, bias):
        start = pl.multiple_of(kt * t, t)
        k = k_ref[pl.ds(start, t), :]
        v = v_ref[pl.ds(start, t), :]
        s = lax.dot_general(qq, k, (((1,), (1,)), ((), ())), preferred_element_type=F32)
        if bias is not None:
            s = s + jnp.concatenate([bias, bias], axis=0)
        m_old = m_ref[...]
        m_new = jnp.maximum(m_old, jnp.max(s, axis=-1, keepdims=True))
        alpha = jnp.exp2(m_old - m_new)
        p = jnp.exp2(s - m_new)
        l_ref[...] = alpha * l_ref[...] + jnp.sum(p, axis=-1, keepdims=True)
        acc_ref[...] = alpha * acc_ref[...] + jnp.dot(p.astype(BF16), v, preferred_element_type=F32)
        m_ref[...] = m_new

    def far_body(kt, carry):
        step(kt, None)
        return carry

    lax.fori_loop(0, jnp.maximum(qi - 1, 0), far_body, 0)

    @pl.when(qi > 0)
    def _():
        step(qi - 1, bias_ref[0, :, :t])

    step(qi, bias_ref[0, :, t:])
    _attn_finish(acc_ref[...], l_ref[...], lam_ref[0], sw_ref[...], out_scale, t, o_ref)


def _attention(lam, shift, qkv, bias, subln_w, n_heads, tile, out_scale):
    s = qkv.shape[0]
    smem = pl.BlockSpec(memory_space=pltpu.SMEM)
    in_specs = [pl.BlockSpec((tile, LANES), lambda h, i: (i, h)),
                pl.BlockSpec((s, LANES), lambda h, i: (0, n_heads + h)),
                pl.BlockSpec((s, LANES), lambda h, i: (0, 2 * n_heads + h)),
                pl.BlockSpec((1, tile, 2 * tile), lambda h, i: (h, 0, 0)),
                pl.BlockSpec((1, ATT_V_DIM), lambda h, i: (0, 0))]
    args = (qkv, qkv, qkv, bias, subln_w.reshape(1, ATT_V_DIM))
    acc = pltpu.VMEM((2 * tile, ATT_V_DIM), F32)
    if shift is not None:
        kernel = functools.partial(_attn_fixed_kernel, tile=tile, far_widths=ATT_FAR_WIDTHS, out_scale=out_scale)
        in_specs = [smem, smem] + in_specs
        args = (lam, shift) + args
        scratch = [pltpu.VMEM((2 * tile, LANES), F32), acc]
        name = "diff_attention"
    else:
        kernel = functools.partial(_attn_online_kernel, tile=tile, out_scale=out_scale)
        in_specs = [smem] + in_specs
        args = (lam,) + args
        scratch = [pltpu.VMEM((2 * tile, 1), F32), pltpu.VMEM((2 * tile, 1), F32), acc]
        name = "diff_attention_online"
    return pl.pallas_call(
        kernel,
        out_shape=jax.ShapeDtypeStruct((s, n_heads * ATT_V_DIM), BF16),
        grid=(n_heads, s // tile),
        in_specs=in_specs,
        out_specs=pl.BlockSpec((tile, ATT_V_DIM), lambda h, i: (i, h)),
        scratch_shapes=scratch,
        compiler_params=_params("arbitrary", "arbitrary"),
        name=name,
    )(*args)


def kernel(x, norm_mix_w, norm_mlp_w, ssm_in_w, ssm_conv_w, ssm_conv_b, ssm_dt_bias, ssm_a_log, ssm_d, ssm_norm_w, ssm_out_w, att_in_w, att_q_norm_w, att_k_norm_w, att_lam_q1, att_lam_k1, att_lam_q2, att_lam_k2, att_subln_w, att_out_w, rel_bias, mlp_w1, mlp_w2):
    bsz, s, d = x.shape
    assert bsz == 1
    depth = norm_mix_w.shape[0]
    d_inner = ssm_out_w.shape[1]
    n_ssm_heads = ssm_dt_bias.shape[1]
    conv_dim = ssm_conv_w.shape[2]
    n_att_heads = rel_bias.shape[1]
    r = n_ssm_heads // SSM_GROUPS

    tm = min(ROW_TILE, s)
    tmr = min(RES_ROW_TILE, s)
    chunk = min(SSD_CHUNK, s)
    tile = min(ATT_TILE, s)
    assert s % tm == 0 and s % tmr == 0 and s % chunk == 0 and s % tile == 0
    assert tile % ATT_CHUNK == 0 and tile > 90

    h = x.reshape(s, d)
    u = _rmsnorm(h, norm_mix_w[0], tmr)

    ti = jnp.arange(tile, dtype=jnp.int32)[:, None]
    tj = jnp.arange(2 * tile, dtype=jnp.int32)[None, :] - tile
    allowed = (tj < 0) | ((tj // ATT_CHUNK) <= (ti // ATT_CHUNK))
    buckets = jnp.where(allowed, _t5_bucket(tj - ti), -1)
    far_bias = rel_bias[_t5_bucket(jnp.int32(-2 * tile))]
    bias = _bias_tiles(rel_bias, far_bias, buckets)
    near_bias = rel_bias.astype(F32) - far_bias[None, :]
    near_hi = jnp.maximum(jnp.max(near_bias, axis=0), 0.0)
    near_lo = jnp.minimum(jnp.min(near_bias, axis=0), 0.0)

    for i in range(depth):
        j = i // 2
        if i % 2 == 0:
            w_in = ssm_in_w[j]
            w_zx = w_in[:, :d_inner + conv_dim].astype(BF16)
            w_dt = w_in[:, d_inner + conv_dim:].astype(BF16)
            zx = _ssm_in_proj(u, w_zx, ssm_conv_w[j], ssm_conv_b[j].reshape(1, conv_dim), d_inner, tm, COL_TILE)
            a = -jnp.exp(ssm_a_log[j].astype(F32))
            mult = jnp.concatenate([jnp.ones_like(a), a]).reshape(1, -1)
            dt_b = jnp.tile(ssm_dt_bias[j].astype(F32), 2).reshape(1, -1)
            nar = _dt_proj(u, jnp.concatenate([w_dt, w_dt], axis=1), dt_b, mult, chunk, tm)
            cols = nar.reshape(s, 2, SSM_GROUPS, r).transpose(2, 0, 1, 3).reshape(SSM_GROUPS, s, 2 * r)
            rows = cols.transpose(0, 2, 1)
            d_exp = jnp.repeat(ssm_d[j].astype(F32), SSM_HEAD_DIM).reshape(1, d_inner)
            y = _ssd(zx, cols, rows, d_exp, ssm_norm_w[j].reshape(1, d_inner), d_inner, chunk)
            h, u = _proj_res(y, ssm_out_w[j].astype(BF16), h, norm_mlp_w[i], tmr, d_inner // 2)
        else:
            lambda_init = 0.8 - 0.6 * math.exp(-0.3 * i)
            lam = (jnp.exp(jnp.sum(att_lam_q1[j].astype(F32) * att_lam_k1[j].astype(F32)))
                   - jnp.exp(jnp.sum(att_lam_q2[j].astype(F32) * att_lam_k2[j].astype(F32)))
                   + lambda_init).reshape(1)
            reps = d // ATT_QK_DIM
            wq = att_q_norm_w[j].astype(F32)
            wk = att_k_norm_w[j].astype(F32)
            qk_w = jnp.concatenate([jnp.tile(wq, reps) * (ATT_QK_DIM ** -0.5 * LOG2E),
                                    jnp.tile(wk, reps)]).reshape(1, 2 * d)
            qkv = _att_in_proj(u, att_in_w[j].astype(BF16), qk_w, 2 * d, tm, COL_TILE)
            logit_bound = (ATT_QK_DIM ** 0.5) * jnp.max(jnp.abs(wq * wk))
            shift = (logit_bound + near_hi) * LOG2E
            spread = jnp.max((2.0 * logit_bound + near_hi - near_lo) * LOG2E)
            att_args = (qkv, bias, att_subln_w[j], n_att_heads, tile, 1.0 - lambda_init)
            o = lax.cond(spread <= EXP2_SAFE_RANGE,
                         lambda: _attention(lam, shift, *att_args),
                         lambda: _attention(lam, None, *att_args))
            h, u = _proj_res(o, att_out_w[j].astype(BF16), h, norm_mlp_w[i], tmr, d)
        nw_next = norm_mix_w[i + 1] if i + 1 < depth else None
        h, u = _mlp(u, mlp_w1[i].astype(BF16), mlp_w2[i].astype(BF16), h, nw_next, tmr, MLP_HIDDEN_TILE)
    return h.reshape(bsz, s, d)
```

```python
import functools
import math

import jax
import jax.numpy as jnp
from jax import lax
from jax.experimental import pallas as pl
from jax.experimental.pallas import tpu as pltpu

F32 = jnp.float32
BF16 = jnp.bfloat16

EPS = 1e-6
MASK_VALUE = -1e30

SSM_HEAD_DIM = 64
SSM_GROUPS = 8
SSM_STATE = 128
SSM_CONV = 4
ATT_QK_DIM = 64
ATT_V_DIM = 128
ATT_CHUNK = 64
NUM_BUCKETS = 32
MAX_DISTANCE = 128

LANES = 128
SUBLANES = 8
MXU_COLS = 512
VMEM_LIMIT_BYTES = 56 * 1024 * 1024

ROW_TILE = 1024
COL_TILE = 1024
RES_ROW_TILE = 512
MLP_HIDDEN_TILE = 512
SSD_CHUNK = 256
ATT_TILE = 512
ATT_FAR_WIDTHS = (4, 2, 1)

LOG2E = 1.4426950408889634
EXP2_SAFE_RANGE = 120.0
NORM_SLAB = 256


def _params(*sem, flags=None):
    return pltpu.CompilerParams(dimension_semantics=sem, vmem_limit_bytes=VMEM_LIMIT_BYTES, flags=flags)


def _silu_from_half(h):
    return h + h * jnp.tanh(h)


def _rms_rows(x, w):
    ms = jnp.mean(x * x, axis=-1, keepdims=True)
    return x * lax.rsqrt(ms + EPS) * w


def _rmsnorm_kernel(x_ref, w_ref, o_ref):
    o_ref[...] = _rms_rows(x_ref[...], w_ref[...]).astype(o_ref.dtype)


def _rmsnorm(x, w, tm):
    s, d = x.shape
    return pl.pallas_call(
        _rmsnorm_kernel,
        out_shape=jax.ShapeDtypeStruct((s, d), BF16),
        grid=(s // tm,),
        in_specs=[pl.BlockSpec((tm, d), lambda i: (i, 0)),
                  pl.BlockSpec((1, d), lambda i: (0, 0))],
        out_specs=pl.BlockSpec((tm, d), lambda i: (i, 0)),
        compiler_params=_params("arbitrary"),
        name="rmsnorm_in",
    )(x, w.reshape(1, d))


def _proj_kernel(u_ref, w_ref, o_ref):
    o_ref[...] = jnp.dot(u_ref[...], w_ref[...], preferred_element_type=F32).astype(o_ref.dtype)


def _proj(u, w, col0, n_out, tm, tn):
    s, d = u.shape
    c0 = col0 // tn
    return pl.pallas_call(
        _proj_kernel,
        out_shape=jax.ShapeDtypeStruct((s, n_out), BF16),
        grid=(n_out // tn, s // tm),
        in_specs=[pl.BlockSpec((tm, d), lambda n, m: (m, 0)),
                  pl.BlockSpec((d, tn), lambda n, m: (0, c0 + n))],
        out_specs=pl.BlockSpec((tm, tn), lambda n, m: (m, n)),
        compiler_params=_params("arbitrary", "arbitrary"),
        name="ssm_z_proj",
    )(u, w)


def _ssm_xbc_kernel(u_ref, w_ref, cw_ref, cb_ref, o_ref, carry_ref):
    tm = u_ref.shape[0]
    groups = tm // SUBLANES

    @pl.when(pl.program_id(1) == 0)
    def _():
        carry_ref[...] = jnp.zeros_like(carry_ref)

    u = u_ref[...]
    sub = lax.broadcasted_iota(jnp.int32, (1, SUBLANES, MXU_COLS), 1)
    for c in range(o_ref.shape[1] // MXU_COLS):
        cols = slice(c * MXU_COLS, (c + 1) * MXU_COLS)
        acc = jnp.dot(u, w_ref[:, cols], preferred_element_type=F32)
        cw = cw_ref[:, cols]
        a3 = jnp.concatenate([carry_ref[:, cols], acc], axis=0).reshape(groups + 1, SUBLANES, MXU_COLS)
        half = a3[1:] * cw[SSM_CONV - 1:SSM_CONV, :] + cb_ref[:, cols]
        rot = a3
        for j in range(1, SSM_CONV):
            rot = pltpu.roll(rot, 1, axis=1)
            shifted = jnp.where(sub >= j, rot[1:], rot[:-1])
            half = half + shifted * cw[SSM_CONV - 1 - j:SSM_CONV - j, :]
        o_ref[:, cols] = _silu_from_half(half).reshape(tm, MXU_COLS).astype(o_ref.dtype)
        carry_ref[:, cols] = acc[tm - SUBLANES:, :]


def _ssm_xbc_proj(u, w, conv_w, conv_b, col0, tm, tn):
    s, d = u.shape
    n_out = conv_w.shape[1]
    c0 = col0 // tn
    return pl.pallas_call(
        _ssm_xbc_kernel,
        out_shape=jax.ShapeDtypeStruct((s, n_out), BF16),
        grid=(n_out // tn, s // tm),
        in_specs=[pl.BlockSpec((tm, d), lambda n, m: (m, 0)),
                  pl.BlockSpec((d, tn), lambda n, m: (0, c0 + n)),
                  pl.BlockSpec((SSM_CONV, tn), lambda n, m: (0, n)),
                  pl.BlockSpec((1, tn), lambda n, m: (0, n))],
        out_specs=pl.BlockSpec((tm, tn), lambda n, m: (m, n)),
        scratch_shapes=[pltpu.VMEM((SUBLANES, tn), F32)],
        compiler_params=_params("arbitrary", "arbitrary"),
        name="ssm_xbc_proj",
    )(u, w, conv_w, conv_b)


def _dt_kernel(u_ref, w_ref, b_ref, mult_ref, o_ref, *, chunk, n_heads):
    x = jnp.dot(u_ref[...], w_ref[...], preferred_element_type=F32) + b_ref[...]
    sp = jnp.maximum(x, 0.0) + jnp.log1p(jnp.exp(-jnp.abs(x)))
    v = sp * mult_ref[...]
    ri = lax.broadcasted_iota(jnp.int32, (chunk, chunk), 0)
    ci = lax.broadcasted_iota(jnp.int32, (chunk, chunk), 1)
    tri = (ri >= ci).astype(F32)
    is_da = lax.broadcasted_iota(jnp.int32, (1, v.shape[1]), 1) >= n_heads
    for c in range(v.shape[0] // chunk):
        blk = v[c * chunk:(c + 1) * chunk]
        cs = jnp.dot(tri, blk, precision=lax.Precision.HIGHEST, preferred_element_type=F32)
        o_ref[c * chunk:(c + 1) * chunk, :] = jnp.where(is_da, cs, blk)


def _dt_proj(u, w2, b2, mult, chunk, tm):
    s, d = u.shape
    n = w2.shape[1]
    kernel = functools.partial(_dt_kernel, chunk=chunk, n_heads=n // 2)
    return pl.pallas_call(
        kernel,
        out_shape=jax.ShapeDtypeStruct((s, n), F32),
        grid=(s // tm,),
        in_specs=[pl.BlockSpec((tm, d), lambda i: (i, 0)),
                  pl.BlockSpec((d, n), lambda i: (0, 0)),
                  pl.BlockSpec((1, n), lambda i: (0, 0)),
                  pl.BlockSpec((1, n), lambda i: (0, 0))],
        out_specs=pl.BlockSpec((tm, n), lambda i: (i, 0)),
        compiler_params=_params("arbitrary"),
        name="ssm_dt_proj",
    )(u, w2, b2, mult)


def _ssd_kernel(z_ref, x_ref, b_ref, c_ref, cols_ref, rows_ref, d_ref, nw_ref, o_ref, st_ref,
                *, chunk, heads_per_group):
    @pl.when(pl.program_id(1) == 0)
    def _():
        st_ref[...] = jnp.zeros_like(st_ref)

    r = heads_per_group
    x = x_ref[...]
    bm = b_ref[...]
    cm = c_ref[...]
    bm32 = bm.astype(F32)
    cm32 = cm.astype(F32)
    cols = cols_ref[0]
    rows = rows_ref[0]
    cb = lax.dot_general(cm, bm, (((1,), (1,)), ((), ())), preferred_element_type=F32)
    ri = lax.broadcasted_iota(jnp.int32, (chunk, chunk), 0)
    ci = lax.broadcasted_iota(jnp.int32, (chunk, chunk), 1)
    causal = ri >= ci
    lo = lax.broadcasted_iota(jnp.int32, (1, LANES), 1) < SSM_HEAD_DIM
    zero16 = jnp.zeros((), BF16)

    ys = []
    for p in range(r // 2):
        ms, cs, bw, dec = [], [], [], []
        for jj in range(2):
            j = 2 * p + jj
            a_col = cols[:, r + j:r + j + 1]
            dt_col = cols[:, j:j + 1]
            a_row = rows[r + j:r + j + 1, :]
            dt_row = rows[j:j + 1, :]
            dmat = jnp.exp(jnp.where(causal, a_col - a_row, MASK_VALUE))
            ms.append((cb * dmat * dt_row).astype(BF16))
            cs.append((cm32 * jnp.exp(a_col)).astype(BF16))
            a_last = a_col[chunk - 1:chunk, :]
            bw.append((bm32 * (dt_col * jnp.exp(a_last - a_col))).astype(BF16))
            dec.append(jnp.exp(a_last))
        xp = x[:, p * LANES:(p + 1) * LANES]
        xbd = jnp.concatenate([jnp.where(lo, xp, zero16), jnp.where(lo, zero16, xp)], axis=0)
        st = st_ref[:, p * LANES:(p + 1) * LANES]
        stb = st.astype(BF16)
        stbd = jnp.concatenate([jnp.where(lo, stb, zero16), jnp.where(lo, zero16, stb)], axis=0)
        y = (jnp.dot(jnp.concatenate(ms, axis=1), xbd, preferred_element_type=F32)
             + jnp.dot(jnp.concatenate(cs, axis=1), stbd, preferred_element_type=F32))
        upd = lax.dot_general(jnp.concatenate(bw, axis=0), xbd, (((0,), (0,)), ((), ())),
                              preferred_element_type=F32)
        st_ref[:, p * LANES:(p + 1) * LANES] = st * jnp.where(lo, dec[0], dec[1]) + upd
        ys.append(y)

    y = jnp.concatenate(ys, axis=1) + x.astype(F32) * d_ref[...]
    z = z_ref[...].astype(F32)
    gated = y * _silu_from_half(0.5 * z)
    o_ref[...] = _rms_rows(gated, nw_ref[...]).astype(o_ref.dtype)


def _ssd(z, xbc, cols, rows, d_exp, norm_w, d_inner, chunk):
    s = z.shape[0]
    g = SSM_GROUPS
    gw = d_inner // g
    r = gw // SSM_HEAD_DIM
    n = SSM_STATE
    b_off = d_inner // n
    c_off = b_off + g
    kernel = functools.partial(_ssd_kernel, chunk=chunk, heads_per_group=r)
    return pl.pallas_call(
        kernel,
        out_shape=jax.ShapeDtypeStruct((s, d_inner), BF16),
        grid=(g, s // chunk),
        in_specs=[pl.BlockSpec((chunk, gw), lambda gi, c: (c, gi)),
                  pl.BlockSpec((chunk, gw), lambda gi, c: (c, gi)),
                  pl.BlockSpec((chunk, n), lambda gi, c: (c, b_off + gi)),
                  pl.BlockSpec((chunk, n), lambda gi, c: (c, c_off + gi)),
                  pl.BlockSpec((1, chunk, 2 * r), lambda gi, c: (gi, c, 0)),
                  pl.BlockSpec((1, 2 * r, chunk), lambda gi, c: (gi, 0, c)),
                  pl.BlockSpec((1, gw), lambda gi, c: (0, gi)),
                  pl.BlockSpec((1, gw), lambda gi, c: (0, gi))],
        out_specs=pl.BlockSpec((chunk, gw), lambda gi, c: (c, gi)),
        scratch_shapes=[pltpu.VMEM((n, gw), F32)],
        compiler_params=_params("arbitrary", "arbitrary"),
        name="ssd_scan",
    )(z, xbc, xbc, xbc, cols, rows, d_exp, norm_w)


def _proj_res_kernel(y_ref, w_ref, h_ref, nw_ref, ho_ref, uo_ref, acc_ref):
    k = pl.program_id(1)

    @pl.when(k == 0)
    def _():
        acc_ref[...] = jnp.zeros_like(acc_ref)

    acc_ref[...] += jnp.dot(y_ref[...], w_ref[...], preferred_element_type=F32)

    @pl.when(k == pl.num_programs(1) - 1)
    def _():
        hn = h_ref[...] + acc_ref[...]
        ho_ref[...] = hn
        uo_ref[...] = _rms_rows(hn, nw_ref[...]).astype(uo_ref.dtype)


def _proj_res(y, w, h, norm_w, tm, tk):
    s, kd = y.shape
    d = w.shape[1]
    return pl.pallas_call(
        _proj_res_kernel,
        out_shape=(jax.ShapeDtypeStruct((s, d), F32), jax.ShapeDtypeStruct((s, d), BF16)),
        grid=(s // tm, kd // tk),
        in_specs=[pl.BlockSpec((tm, tk), lambda m, k: (m, k)),
                  pl.BlockSpec((tk, d), lambda m, k: (k, 0)),
                  pl.BlockSpec((tm, d), lambda m, k: (m, 0)),
                  pl.BlockSpec((1, d), lambda m, k: (0, 0))],
        out_specs=(pl.BlockSpec((tm, d), lambda m, k: (m, 0)),
                   pl.BlockSpec((tm, d), lambda m, k: (m, 0))),
        scratch_shapes=[pltpu.VMEM((tm, d), F32)],
        compiler_params=_params("arbitrary", "arbitrary"),
        name="proj_residual",
    )(y, w, h, norm_w.reshape(1, d))


def _mlp_kernel(u_ref, w1_ref, w2_ref, h_ref, *rest, emit_norm):
    if emit_norm:
        nw_ref, ho_ref, uo_ref, acc_ref = rest
    else:
        ho_ref, acc_ref = rest
    j = pl.program_id(1)

    @pl.when(j == 0)
    def _():
        acc_ref[...] = jnp.zeros_like(acc_ref)

    a = jnp.dot(u_ref[...], w1_ref[...], preferred_element_type=F32)
    a = jnp.square(jnp.maximum(a, 0.0)).astype(BF16)
    acc_ref[...] += jnp.dot(a, w2_ref[...], preferred_element_type=F32)

    @pl.when(j == pl.num_programs(1) - 1)
    def _():
        hn = h_ref[...] + acc_ref[...]
        ho_ref[...] = hn
        if emit_norm:
            uo_ref[...] = _rms_rows(hn, nw_ref[...]).astype(uo_ref.dtype)


def _mlp(u, w1, w2, h, norm_w, tm, th):
    s, d = u.shape
    hd = w1.shape[1]
    emit_norm = norm_w is not None
    row_spec = pl.BlockSpec((tm, d), lambda m, j: (m, 0))
    once_spec = pl.BlockSpec((tm, d), lambda m, j: (m, 0), pipeline_mode=pl.Buffered(1))
    in_specs = [row_spec,
                pl.BlockSpec((d, th), lambda m, j: (0, j)),
                pl.BlockSpec((th, d), lambda m, j: (j, 0)),
                once_spec]
    args = [u, w1, w2, h]
    out_shape = [jax.ShapeDtypeStruct((s, d), F32)]
    out_specs = [once_spec]
    if emit_norm:
        in_specs.append(pl.BlockSpec((1, d), lambda m, j: (0, 0)))
        args.append(norm_w.reshape(1, d))
        out_shape.append(jax.ShapeDtypeStruct((s, d), BF16))
        out_specs.append(once_spec)
    out = pl.pallas_call(
        functools.partial(_mlp_kernel, emit_norm=emit_norm),
        out_shape=tuple(out_shape),
        grid=(s // tm, hd // th),
        in_specs=in_specs,
        out_specs=tuple(out_specs),
        scratch_shapes=[pltpu.VMEM((tm, d), F32)],
        compiler_params=_params("arbitrary", "arbitrary"),
        name="mlp_residual",
    )(*args)
    return out if emit_norm else (out[0], None)


def _att_in_kernel(u_ref, w_ref, nw_ref, o_ref, *, n_qk_tiles):
    n = pl.program_id(0)
    acc = jnp.dot(u_ref[...], w_ref[...], preferred_element_type=F32)

    @pl.when(n >= n_qk_tiles)
    def _():
        o_ref[...] = acc.astype(o_ref.dtype)

    @pl.when(n < n_qk_tiles)
    def _():
        gi = lax.broadcasted_iota(jnp.int32, (NORM_SLAB, NORM_SLAB), 0) // ATT_QK_DIM
        gj = lax.broadcasted_iota(jnp.int32, (NORM_SLAB, NORM_SLAB), 1) // ATT_QK_DIM
        ones_bd = (gi == gj).astype(BF16)
        nw = nw_ref[...]
        for sl in range(acc.shape[1] // NORM_SLAB):
            a = acc[:, sl * NORM_SLAB:(sl + 1) * NORM_SLAB]
            ss = jnp.dot((a * a).astype(BF16), ones_bd, preferred_element_type=F32)
            out = a * lax.rsqrt(ss * (1.0 / ATT_QK_DIM) + EPS) * nw[:, sl * NORM_SLAB:(sl + 1) * NORM_SLAB]
            o_ref[:, sl * NORM_SLAB:(sl + 1) * NORM_SLAB] = out.astype(o_ref.dtype)


def _att_in_proj(u, w, qk_w, n_qk, tm, tn):
    s, d = u.shape
    n_out = w.shape[1]
    n_qk_tiles = n_qk // tn
    kernel = functools.partial(_att_in_kernel, n_qk_tiles=n_qk_tiles)
    return pl.pallas_call(
        kernel,
        out_shape=jax.ShapeDtypeStruct((s, n_out), BF16),
        grid=(n_out // tn, s // tm),
        in_specs=[pl.BlockSpec((tm, d), lambda n, m: (m, 0)),
                  pl.BlockSpec((d, tn), lambda n, m: (0, n)),
                  pl.BlockSpec((1, tn), lambda n, m: (0, jnp.minimum(n, n_qk_tiles - 1)))],
        out_specs=pl.BlockSpec((tm, tn), lambda n, m: (m, n)),
        compiler_params=_params("arbitrary", "arbitrary"),
        name="att_in_proj",
    )(u, w, qk_w)


def _bias_kernel(rel_ref, far_ref, bk_ref, o_ref):
    h = pl.program_id(0)
    bk = bk_ref[...]
    far = far_ref[h]
    acc = jnp.zeros(bk.shape, F32)
    for b in range(NUM_BUCKETS):
        acc = jnp.where(bk == b, (rel_ref[b, h] - far) * LOG2E, acc)
    o_ref[0] = jnp.where(bk < 0, MASK_VALUE, acc)


def _bias_tiles(rel_bias, far_bias, buckets):
    nb, nh = rel_bias.shape
    t, t2 = buckets.shape
    return pl.pallas_call(
        _bias_kernel,
        out_shape=jax.ShapeDtypeStruct((nh, t, t2), F32),
        grid=(nh,),
        in_specs=[pl.BlockSpec(memory_space=pltpu.SMEM),
                  pl.BlockSpec(memory_space=pltpu.SMEM),
                  pl.BlockSpec((t, t2), lambda h: (0, 0))],
        out_specs=pl.BlockSpec((1, t, t2), lambda h: (h, 0, 0)),
        compiler_params=_params("arbitrary"),
        name="att_bias_tiles",
    )(rel_bias, far_bias, buckets)


def _t5_bucket(rel):
    nb = NUM_BUCKETS // 2
    ret = (rel > 0).astype(jnp.int32) * nb
    n = jnp.abs(rel)
    max_exact = nb // 2
    nf = jnp.maximum(n, max_exact).astype(F32)
    large = max_exact + (jnp.log(nf / max_exact) / math.log(MAX_DISTANCE / max_exact)
                         * (nb - max_exact)).astype(jnp.int32)
    large = jnp.minimum(large, nb - 1)
    return ret + jnp.where(n < max_exact, n, large)


def _stack_maps(q):
    lo = lax.broadcasted_iota(jnp.int32, (1, LANES), 1) < ATT_QK_DIM
    zero16 = jnp.zeros((), BF16)
    return jnp.concatenate([jnp.where(lo, q, zero16), jnp.where(lo, zero16, q)], axis=0)


def _attn_finish(acc, l, lam, sw, out_scale, t, o_ref):
    o = acc / l
    o = o[:t] - lam * o[t:]
    o_ref[...] = (_rms_rows(o, sw) * out_scale).astype(o_ref.dtype)


def _attn_fixed_kernel(lam_ref, shift_ref, q_ref, k_ref, v_ref, bias_ref, sw_ref, o_ref, l_ref, acc_ref,
                       *, tile, far_widths, out_scale):
    shift = shift_ref[pl.program_id(0)]
    qi = pl.program_id(1)
    t = tile
    qq = _stack_maps(q_ref[...])
    l_ref[...] = jnp.zeros_like(l_ref)
    acc_ref[...] = jnp.zeros_like(acc_ref)

    def step(first_tile, n_tiles, bias):
        width = n_tiles * t
        start = pl.multiple_of(first_tile * t, t)
        k = k_ref[pl.ds(start, width), :]
        v = v_ref[pl.ds(start, width), :]
        s = lax.dot_general(qq, k, (((1,), (1,)), ((), ())), preferred_element_type=F32)
        if bias is None:
            p = jnp.exp2(s - shift)
        else:
            b = bias - shift
            p = jnp.exp2(s + jnp.concatenate([b, b], axis=0))
        part = p[:, :LANES]
        for c in range(1, width // LANES):
            part = part + p[:, c * LANES:(c + 1) * LANES]
        l_ref[...] += part
        acc_ref[...] += jnp.dot(p.astype(BF16), v, preferred_element_type=F32)

    n_far = jnp.maximum(qi - 1, 0)
    w0 = far_widths[0]

    def wide_body(i, carry):
        step(i * w0, w0, None)
        return carry

    lax.fori_loop(0, n_far // w0, wide_body, 0)
    for w in far_widths[1:]:
        @pl.when((n_far & w) != 0)
        def _(w=w):
            step(n_far & ~(2 * w - 1), w, None)

    @pl.when(qi > 0)
    def _():
        step(qi - 1, 2, bias_ref[0])

    @pl.when(qi == 0)
    def _():
        step(0, 1, bias_ref[0, :, t:])

    l = jnp.sum(l_ref[...], axis=-1, keepdims=True)
    _attn_finish(acc_ref[...], l, lam_ref[0], sw_ref[...], out_scale, t, o_ref)


def _attn_online_kernel(lam_ref, q_ref, k_ref, v_ref, bias_ref, sw_ref, o_ref, m_ref, l_ref, acc_ref,
                        *, tile, out_scale):
    qi = pl.program_id(1)
    t = tile
    qq = _stack_maps(q_ref[...])
    m_ref[...] = jnp.full_like(m_ref, MASK_VALUE)
    l_ref[...] = jnp.zeros_like(l_ref)
    acc_ref[...] = jnp.zeros_like(acc_ref)

    def step(kt, bias):
        start = pl.multiple_of(kt * t, t)
        k = k_ref[pl.ds(start, t), :]
        v = v_ref[pl.ds(start, t), :]
        s = lax.dot_general(qq, k, (((1,), (1,)), ((), ())), preferred_element_type=F32)
        if bias is not None:
            s = s + jnp.concatenate([bias, bias], axis=0)
        m_old = m_ref[...]
        m_new = jnp.maximum(m_old, jnp.max(s, axis=-1, keepdims=True))
        alpha = jnp.exp2(m_old - m_new)
        p = jnp.exp2(s - m_new)
        l_ref[...] = alpha * l_ref[...] + jnp.sum(p, axis=-1, keepdims=True)
        acc_ref[...] = alpha * acc_ref[...] + jnp.dot(p.astype(BF16), v, preferred_element_type=F32)
        m_ref[...] = m_new

    def far_body(kt, carry):
        step(kt, None)
        return carry

    lax.fori_loop(0, jnp.maximum(qi - 1, 0), far_body, 0)

    @pl.when(qi > 0)
    def _():
        step(qi - 1, bias_ref[0, :, :t])

    step(qi, bias_ref[0, :, t:])
    _attn_finish(acc_ref[...], l_ref[...], lam_ref[0], sw_ref[...], out_scale, t, o_ref)


def _attention(lam, shift, qkv, bias, subln_w, n_heads, tile, out_scale):
    s = qkv.shape[0]
    smem = pl.BlockSpec(memory_space=pltpu.SMEM)
    in_specs = [pl.BlockSpec((tile, LANES), lambda h, i: (i, h)),
                pl.BlockSpec((s, LANES), lambda h, i: (0, n_heads + h)),
                pl.BlockSpec((s, LANES), lambda h, i: (0, 2 * n_heads + h)),
                pl.BlockSpec((1, tile, 2 * tile), lambda h, i: (h, 0, 0)),
                pl.BlockSpec((1, ATT_V_DIM), lambda h, i: (0, 0))]
    args = (qkv, qkv, qkv, bias, subln_w.reshape(1, ATT_V_DIM))
    acc = pltpu.VMEM((2 * tile, ATT_V_DIM), F32)
    if shift is not None:
        kernel = functools.partial(_attn_fixed_kernel, tile=tile, far_widths=ATT_FAR_WIDTHS, out_scale=out_scale)
        in_specs = [smem, smem] + in_specs
        args = (lam, shift) + args
        scratch = [pltpu.VMEM((2 * tile, LANES), F32), acc]
        name = "diff_attention"
    else:
        kernel = functools.partial(_attn_online_kernel, tile=tile, out_scale=out_scale)
        in_specs = [smem] + in_specs
        args = (lam,) + args
        scratch = [pltpu.VMEM((2 * tile, 1), F32), pltpu.VMEM((2 * tile, 1), F32), acc]
        name = "diff_attention_online"
    return pl.pallas_call(
        kernel,
        out_shape=jax.ShapeDtypeStruct((s, n_heads * ATT_V_DIM), BF16),
        grid=(n_heads, s // tile),
        in_specs=in_specs,
        out_specs=pl.BlockSpec((tile, ATT_V_DIM), lambda h, i: (i, h)),
        scratch_shapes=scratch,
        compiler_params=_params("arbitrary", "arbitrary"),
        name=name,
    )(*args)


def kernel(x, norm_mix_w, norm_mlp_w, ssm_in_w, ssm_conv_w, ssm_conv_b, ssm_dt_bias, ssm_a_log, ssm_d, ssm_norm_w, ssm_out_w, att_in_w, att_q_norm_w, att_k_norm_w, att_lam_q1, att_lam_k1, att_lam_q2, att_lam_k2, att_subln_w, att_out_w, rel_bias, mlp_w1, mlp_w2):
    bsz, s, d = x.shape
    assert bsz == 1
    depth = norm_mix_w.shape[0]
    d_inner = ssm_out_w.shape[1]
    n_ssm_heads = ssm_dt_bias.shape[1]
    conv_dim = ssm_conv_w.shape[2]
    n_att_heads = rel_bias.shape[1]
    r = n_ssm_heads // SSM_GROUPS

    tm = min(ROW_TILE, s)
    tmr = min(RES_ROW_TILE, s)
    chunk = min(SSD_CHUNK, s)
    tile = min(ATT_TILE, s)
    assert s % tm == 0 and s % tmr == 0 and s % chunk == 0 and s % tile == 0
    assert tile % ATT_CHUNK == 0 and tile > 90

    h = x.reshape(s, d)
    u = _rmsnorm(h, norm_mix_w[0], tmr)

    ti = jnp.arange(tile, dtype=jnp.int32)[:, None]
    tj = jnp.arange(2 * tile, dtype=jnp.int32)[None, :] - tile
    allowed = (tj < 0) | ((tj // ATT_CHUNK) <= (ti // ATT_CHUNK))
    buckets = jnp.where(allowed, _t5_bucket(tj - ti), -1)
    far_bias = rel_bias[_t5_bucket(jnp.int32(-2 * tile))]
    bias = _bias_tiles(rel_bias, far_bias, buckets)
    near_bias = rel_bias.astype(F32) - far_bias[None, :]
    near_hi = jnp.maximum(jnp.max(near_bias, axis=0), 0.0)
    near_lo = jnp.minimum(jnp.min(near_bias, axis=0), 0.0)

    for i in range(depth):
        j = i // 2
        if i % 2 == 0:
            w_in = ssm_in_w[j].astype(BF16)
            w_dt = w_in[:, d_inner + conv_dim:]
            z = _proj(u, w_in, 0, d_inner, tm, COL_TILE)
            xbc = _ssm_xbc_proj(u, w_in, 0.5 * ssm_conv_w[j], 0.5 * ssm_conv_b[j].reshape(1, conv_dim), d_inner, tm, 2 * COL_TILE)
            a = -jnp.exp(ssm_a_log[j].astype(F32))
            mult = jnp.concatenate([jnp.ones_like(a), a]).reshape(1, -1)
            dt_b = jnp.tile(ssm_dt_bias[j].astype(F32), 2).reshape(1, -1)
            nar = _dt_proj(u, jnp.concatenate([w_dt, w_dt], axis=1), dt_b, mult, chunk, tm)
            cols = nar.reshape(s, 2, SSM_GROUPS, r).transpose(2, 0, 1, 3).reshape(SSM_GROUPS, s, 2 * r)
            rows = cols.transpose(0, 2, 1)
            d_exp = jnp.repeat(ssm_d[j].astype(F32), SSM_HEAD_DIM).reshape(1, d_inner)
            y = _ssd(z, xbc, cols, rows, d_exp, ssm_norm_w[j].reshape(1, d_inner), d_inner, chunk)
            h, u = _proj_res(y, ssm_out_w[j].astype(BF16), h, norm_mlp_w[i], tmr, d_inner // 2)
        else:
            lambda_init = 0.8 - 0.6 * math.exp(-0.3 * i)
            lam = (jnp.exp(jnp.sum(att_lam_q1[j].astype(F32) * att_lam_k1[j].astype(F32)))
                   - jnp.exp(jnp.sum(att_lam_q2[j].astype(F32) * att_lam_k2[j].astype(F32)))
                   + lambda_init).reshape(1)
            reps = d // ATT_QK_DIM
            wq = att_q_norm_w[j].astype(F32)
            wk = att_k_norm_w[j].astype(F32)
            qk_w = jnp.concatenate([jnp.tile(wq, reps) * (ATT_QK_DIM ** -0.5 * LOG2E),
                                    jnp.tile(wk, reps)]).reshape(1, 2 * d)
            qkv = _att_in_proj(u, att_in_w[j].astype(BF16), qk_w, 2 * d, tm, COL_TILE)
            logit_bound = (ATT_QK_DIM ** 0.5) * jnp.max(jnp.abs(wq * wk))
            shift = (logit_bound + near_hi) * LOG2E
            spread = jnp.max((2.0 * logit_bound + near_hi - near_lo) * LOG2E)
            att_args = (qkv, bias, att_subln_w[j], n_att_heads, tile, 1.0 - lambda_init)
            o = lax.cond(spread <= EXP2_SAFE_RANGE,
                         lambda: _attention(lam, shift, *att_args),
                         lambda: _attention(lam, None, *att_args))
            h, u = _proj_res(o, att_out_w[j].astype(BF16), h, norm_mlp_w[i], tmr, d)
        nw_next = norm_mix_w[i + 1] if i + 1 < depth else None
        h, u = _mlp(u, mlp_w1[i].astype(BF16), mlp_w2[i].astype(BF16), h, nw_next, tm, MLP_HIDDEN_TILE)
    return h.reshape(bsz, s, d)
```

```python
import functools
import math

import jax
import jax.numpy as jnp
from jax import lax
from jax.experimental import pallas as pl
from jax.experimental.pallas import tpu as pltpu

F32 = jnp.float32
BF16 = jnp.bfloat16

EPS = 1e-6
MASK_VALUE = -1e30

SSM_HEAD_DIM = 64
SSM_GROUPS = 8
SSM_STATE = 128
SSM_CONV = 4
ATT_QK_DIM = 64
ATT_V_DIM = 128
ATT_CHUNK = 64
NUM_BUCKETS = 32
MAX_DISTANCE = 128

LANES = 128
SUBLANES = 8
MXU_COLS = 512
VMEM_LIMIT_BYTES = 56 * 1024 * 1024

ROW_TILE = 1024
COL_TILE = 1024
RES_ROW_TILE = 512
MLP_HIDDEN_TILE = 512
SSD_CHUNK = 256
ATT_TILE = 512
ATT_FAR_WIDTHS = (4, 2, 1)

LOG2E = 1.4426950408889634
EXP2_SAFE_RANGE = 120.0
NORM_SLAB = 256


def _params(*sem):
    return pltpu.CompilerParams(dimension_semantics=sem, vmem_limit_bytes=VMEM_LIMIT_BYTES)


def _silu_from_half(h):
    return h + h * jnp.tanh(h)


def _rms_rows(x, w):
    ms = jnp.mean(x * x, axis=-1, keepdims=True)
    return x * lax.rsqrt(ms + EPS) * w


def _rmsnorm_kernel(x_ref, w_ref, o_ref):
    o_ref[...] = _rms_rows(x_ref[...], w_ref[...]).astype(o_ref.dtype)


def _rmsnorm(x, w, tm):
    s, d = x.shape
    return pl.pallas_call(
        _rmsnorm_kernel,
        out_shape=jax.ShapeDtypeStruct((s, d), BF16),
        grid=(s // tm,),
        in_specs=[pl.BlockSpec((tm, d), lambda i: (i, 0)),
                  pl.BlockSpec((1, d), lambda i: (0, 0))],
        out_specs=pl.BlockSpec((tm, d), lambda i: (i, 0)),
        compiler_params=_params("arbitrary"),
        name="rmsnorm_in",
    )(x, w.reshape(1, d))


def _cast_weight_tile(w_ref, wb_ref):
    @pl.when(pl.program_id(1) == 0)
    def _():
        wb_ref[...] = w_ref[...].astype(BF16)


def _weight_tile_spec(layer, d, tn, c0):
    return pl.BlockSpec((None, d, tn), lambda n, m: (layer, 0, c0 + n))


def _proj_kernel(u_ref, w_ref, o_ref, wb_ref):
    _cast_weight_tile(w_ref, wb_ref)
    o_ref[...] = jnp.dot(u_ref[...], wb_ref[...], preferred_element_type=F32).astype(o_ref.dtype)


def _proj(u, w, layer, col0, n_out, tm, tn):
    s, d = u.shape
    return pl.pallas_call(
        _proj_kernel,
        out_shape=jax.ShapeDtypeStruct((s, n_out), BF16),
        grid=(n_out // tn, s // tm),
        in_specs=[pl.BlockSpec((tm, d), lambda n, m: (m, 0)),
                  _weight_tile_spec(layer, d, tn, col0 // tn)],
        out_specs=pl.BlockSpec((tm, tn), lambda n, m: (m, n)),
        scratch_shapes=[pltpu.VMEM((d, tn), BF16)],
        compiler_params=_params("arbitrary", "arbitrary"),
        name="ssm_z_proj",
    )(u, w)


def _ssm_xbc_kernel(u_ref, w_ref, cw_ref, cb_ref, o_ref, wb_ref, carry_ref):
    tm = u_ref.shape[0]
    groups = tm // SUBLANES
    _cast_weight_tile(w_ref, wb_ref)

    @pl.when(pl.program_id(1) == 0)
    def _():
        carry_ref[...] = jnp.zeros_like(carry_ref)

    u = u_ref[...]
    sub = lax.broadcasted_iota(jnp.int32, (1, SUBLANES, MXU_COLS), 1)
    for c in range(o_ref.shape[1] // MXU_COLS):
        cols = slice(c * MXU_COLS, (c + 1) * MXU_COLS)
        acc = jnp.dot(u, wb_ref[:, cols], preferred_element_type=F32)
        cw = cw_ref[:, cols]
        a3 = jnp.concatenate([carry_ref[:, cols], acc], axis=0).reshape(groups + 1, SUBLANES, MXU_COLS)
        half = a3[1:] * cw[SSM_CONV - 1:SSM_CONV, :] + cb_ref[:, cols]
        rot = a3
        for j in range(1, SSM_CONV):
            rot = pltpu.roll(rot, 1, axis=1)
            shifted = jnp.where(sub >= j, rot[1:], rot[:-1])
            half = half + shifted * cw[SSM_CONV - 1 - j:SSM_CONV - j, :]
        o_ref[:, cols] = _silu_from_half(half).reshape(tm, MXU_COLS).astype(o_ref.dtype)
        carry_ref[:, cols] = acc[tm - SUBLANES:, :]


def _ssm_xbc_proj(u, w, layer, conv_w, conv_b, col0, tm, tn):
    s, d = u.shape
    n_out = conv_w.shape[1]
    return pl.pallas_call(
        _ssm_xbc_kernel,
        out_shape=jax.ShapeDtypeStruct((s, n_out), BF16),
        grid=(n_out // tn, s // tm),
        in_specs=[pl.BlockSpec((tm, d), lambda n, m: (m, 0)),
                  _weight_tile_spec(layer, d, tn, col0 // tn),
                  pl.BlockSpec((SSM_CONV, tn), lambda n, m: (0, n)),
                  pl.BlockSpec((1, tn), lambda n, m: (0, n))],
        out_specs=pl.BlockSpec((tm, tn), lambda n, m: (m, n)),
        scratch_shapes=[pltpu.VMEM((d, tn), BF16), pltpu.VMEM((SUBLANES, tn), F32)],
        compiler_params=_params("arbitrary", "arbitrary"),
        name="ssm_xbc_proj",
    )(u, w, conv_w, conv_b)


def _dt_kernel(u_ref, w_ref, b_ref, mult_ref, o_ref, *, chunk, n_heads):
    x = jnp.dot(u_ref[...], w_ref[...], preferred_element_type=F32) + b_ref[...]
    sp = jnp.maximum(x, 0.0) + jnp.log1p(jnp.exp(-jnp.abs(x)))
    v = sp * mult_ref[...]
    ri = lax.broadcasted_iota(jnp.int32, (chunk, chunk), 0)
    ci = lax.broadcasted_iota(jnp.int32, (chunk, chunk), 1)
    tri = (ri >= ci).astype(F32)
    is_da = lax.broadcasted_iota(jnp.int32, (1, v.shape[1]), 1) >= n_heads
    for c in range(v.shape[0] // chunk):
        blk = v[c * chunk:(c + 1) * chunk]
        cs = jnp.dot(tri, blk, precision=lax.Precision.HIGHEST, preferred_element_type=F32)
        o_ref[c * chunk:(c + 1) * chunk, :] = jnp.where(is_da, cs, blk)


def _dt_proj(u, w2, b2, mult, chunk, tm):
    s, d = u.shape
    n = w2.shape[1]
    kernel = functools.partial(_dt_kernel, chunk=chunk, n_heads=n // 2)
    return pl.pallas_call(
        kernel,
        out_shape=jax.ShapeDtypeStruct((s, n), F32),
        grid=(s // tm,),
        in_specs=[pl.BlockSpec((tm, d), lambda i: (i, 0)),
                  pl.BlockSpec((d, n), lambda i: (0, 0)),
                  pl.BlockSpec((1, n), lambda i: (0, 0)),
                  pl.BlockSpec((1, n), lambda i: (0, 0))],
        out_specs=pl.BlockSpec((tm, n), lambda i: (i, 0)),
        compiler_params=_params("arbitrary"),
        name="ssm_dt_proj",
    )(u, w2, b2, mult)


def _ssd_kernel(z_ref, x_ref, b_ref, c_ref, cols_ref, rows_ref, d_ref, nw_ref, o_ref, st_ref,
                *, chunk, heads_per_group):
    @pl.when(pl.program_id(1) == 0)
    def _():
        st_ref[...] = jnp.zeros_like(st_ref)

    r = heads_per_group
    x = x_ref[...]
    bm = b_ref[...]
    cm = c_ref[...]
    bm32 = bm.astype(F32)
    cm32 = cm.astype(F32)
    cols = cols_ref[0]
    rows = rows_ref[0]
    cb = lax.dot_general(cm, bm, (((1,), (1,)), ((), ())), preferred_element_type=F32)
    ri = lax.broadcasted_iota(jnp.int32, (chunk, chunk), 0)
    ci = lax.broadcasted_iota(jnp.int32, (chunk, chunk), 1)
    causal = ri >= ci
    lo = lax.broadcasted_iota(jnp.int32, (1, LANES), 1) < SSM_HEAD_DIM
    zero16 = jnp.zeros((), BF16)

    ys = []
    for p in range(r // 2):
        ms, cs, bw, dec = [], [], [], []
        for jj in range(2):
            j = 2 * p + jj
            a_col = cols[:, r + j:r + j + 1]
            dt_col = cols[:, j:j + 1]
            a_row = rows[r + j:r + j + 1, :]
            dt_row = rows[j:j + 1, :]
            dmat = jnp.exp2(jnp.where(causal, a_col - a_row, MASK_VALUE))
            ms.append((cb * dmat * dt_row).astype(BF16))
            cs.append((cm32 * jnp.exp2(a_col)).astype(BF16))
            a_last = a_col[chunk - 1:chunk, :]
            bw.append((bm32 * (dt_col * jnp.exp2(a_last - a_col))).astype(BF16))
            dec.append(jnp.exp2(a_last))
        xp = x[:, p * LANES:(p + 1) * LANES]
        xbd = jnp.concatenate([jnp.where(lo, xp, zero16), jnp.where(lo, zero16, xp)], axis=0)
        st = st_ref[:, p * LANES:(p + 1) * LANES]
        stb = st.astype(BF16)
        stbd = jnp.concatenate([jnp.where(lo, stb, zero16), jnp.where(lo, zero16, stb)], axis=0)
        y = (jnp.dot(jnp.concatenate(ms, axis=1), xbd, preferred_element_type=F32)
             + jnp.dot(jnp.concatenate(cs, axis=1), stbd, preferred_element_type=F32))
        upd = lax.dot_general(jnp.concatenate(bw, axis=0), xbd, (((0,), (0,)), ((), ())),
                              preferred_element_type=F32)
        st_ref[:, p * LANES:(p + 1) * LANES] = st * jnp.where(lo, dec[0], dec[1]) + upd
        ys.append(y)

    y = jnp.concatenate(ys, axis=1) + x.astype(F32) * d_ref[...]
    z = z_ref[...].astype(F32)
    gated = y * _silu_from_half(0.5 * z)
    o_ref[...] = _rms_rows(gated, nw_ref[...]).astype(o_ref.dtype)


def _ssd(z, xbc, cols, rows, d_exp, norm_w, d_inner, chunk):
    s = z.shape[0]
    g = SSM_GROUPS
    gw = d_inner // g
    r = gw // SSM_HEAD_DIM
    n = SSM_STATE
    b_off = d_inner // n
    c_off = b_off + g
    kernel = functools.partial(_ssd_kernel, chunk=chunk, heads_per_group=r)
    return pl.pallas_call(
        kernel,
        out_shape=jax.ShapeDtypeStruct((s, d_inner), BF16),
        grid=(g, s // chunk),
        in_specs=[pl.BlockSpec((chunk, gw), lambda gi, c: (c, gi)),
                  pl.BlockSpec((chunk, gw), lambda gi, c: (c, gi)),
                  pl.BlockSpec((chunk, n), lambda gi, c: (c, b_off + gi)),
                  pl.BlockSpec((chunk, n), lambda gi, c: (c, c_off + gi)),
                  pl.BlockSpec((1, chunk, 2 * r), lambda gi, c: (gi, c, 0)),
                  pl.BlockSpec((1, 2 * r, chunk), lambda gi, c: (gi, 0, c)),
                  pl.BlockSpec((1, gw), lambda gi, c: (0, gi)),
                  pl.BlockSpec((1, gw), lambda gi, c: (0, gi))],
        out_specs=pl.BlockSpec((chunk, gw), lambda gi, c: (c, gi)),
        scratch_shapes=[pltpu.VMEM((n, gw), F32)],
        compiler_params=_params("arbitrary", "arbitrary"),
        name="ssd_scan",
    )(z, xbc, xbc, xbc, cols, rows, d_exp, norm_w)


def _proj_res_kernel(y_ref, w_ref, h_ref, nw_ref, ho_ref, uo_ref, acc_ref):
    k = pl.program_id(1)

    @pl.when(k == 0)
    def _():
        acc_ref[...] = jnp.zeros_like(acc_ref)

    acc_ref[...] += jnp.dot(y_ref[...], w_ref[...], preferred_element_type=F32)

    @pl.when(k == pl.num_programs(1) - 1)
    def _():
        hn = h_ref[...] + acc_ref[...]
        ho_ref[...] = hn
        uo_ref[...] = _rms_rows(hn, nw_ref[...]).astype(uo_ref.dtype)


def _proj_res(y, w, layer, h, norm_w, tm, tk):
    s, kd = y.shape
    d = w.shape[2]
    return pl.pallas_call(
        _proj_res_kernel,
        out_shape=(jax.ShapeDtypeStruct((s, d), F32), jax.ShapeDtypeStruct((s, d), BF16)),
        grid=(s // tm, kd // tk),
        in_specs=[pl.BlockSpec((tm, tk), lambda m, k: (m, k)),
                  pl.BlockSpec((None, tk, d), lambda m, k: (layer, k, 0)),
                  pl.BlockSpec((tm, d), lambda m, k: (m, 0)),
                  pl.BlockSpec((1, d), lambda m, k: (0, 0))],
        out_specs=(pl.BlockSpec((tm, d), lambda m, k: (m, 0)),
                   pl.BlockSpec((tm, d), lambda m, k: (m, 0))),
        scratch_shapes=[pltpu.VMEM((tm, d), F32)],
        compiler_params=_params("arbitrary", "arbitrary"),
        name="proj_residual",
    )(y, w, h, norm_w.reshape(1, d))


def _mlp_kernel(u_ref, w1_ref, w2_ref, h_ref, *rest, emit_norm):
    if emit_norm:
        nw_ref, ho_ref, uo_ref, acc_ref = rest
    else:
        ho_ref, acc_ref = rest
    j = pl.program_id(1)

    @pl.when(j == 0)
    def _():
        acc_ref[...] = jnp.zeros_like(acc_ref)

    a = jnp.dot(u_ref[...], w1_ref[...], preferred_element_type=F32)
    a = jnp.square(jnp.maximum(a, 0.0)).astype(BF16)
    acc_ref[...] += jnp.dot(a, w2_ref[...], preferred_element_type=F32)

    @pl.when(j == pl.num_programs(1) - 1)
    def _():
        hn = h_ref[...] + acc_ref[...]
        ho_ref[...] = hn
        if emit_norm:
            uo_ref[...] = _rms_rows(hn, nw_ref[...]).astype(uo_ref.dtype)


def _mlp(u, w1, w2, layer, h, norm_w, tm, th):
    s, d = u.shape
    hd = w1.shape[2]
    emit_norm = norm_w is not None
    row_spec = pl.BlockSpec((tm, d), lambda m, j: (m, 0))
    once_spec = pl.BlockSpec((tm, d), lambda m, j: (m, 0), pipeline_mode=pl.Buffered(1))
    in_specs = [row_spec,
                pl.BlockSpec((None, d, th), lambda m, j: (layer, 0, j)),
                pl.BlockSpec((None, th, d), lambda m, j: (layer, j, 0)),
                once_spec]
    args = [u, w1, w2, h]
    out_shape = [jax.ShapeDtypeStruct((s, d), F32)]
    out_specs = [once_spec]
    if emit_norm:
        in_specs.append(pl.BlockSpec((1, d), lambda m, j: (0, 0)))
        args.append(norm_w.reshape(1, d))
        out_shape.append(jax.ShapeDtypeStruct((s, d), BF16))
        out_specs.append(once_spec)
    out = pl.pallas_call(
        functools.partial(_mlp_kernel, emit_norm=emit_norm),
        out_shape=tuple(out_shape),
        grid=(s // tm, hd // th),
        in_specs=in_specs,
        out_specs=tuple(out_specs),
        scratch_shapes=[pltpu.VMEM((tm, d), F32)],
        compiler_params=_params("arbitrary", "arbitrary"),
        name="mlp_residual",
    )(*args)
    return out if emit_norm else (out[0], None)


def _att_in_kernel(u_ref, w_ref, nw_ref, o_ref, wb_ref, *, n_qk_tiles):
    n = pl.program_id(0)
    _cast_weight_tile(w_ref, wb_ref)
    acc = jnp.dot(u_ref[...], wb_ref[...], preferred_element_type=F32)

    @pl.when(n >= n_qk_tiles)
    def _():
        o_ref[...] = acc.astype(o_ref.dtype)

    @pl.when(n < n_qk_tiles)
    def _():
        gi = lax.broadcasted_iota(jnp.int32, (NORM_SLAB, NORM_SLAB), 0) // ATT_QK_DIM
        gj = lax.broadcasted_iota(jnp.int32, (NORM_SLAB, NORM_SLAB), 1) // ATT_QK_DIM
        ones_bd = (gi == gj).astype(BF16)
        nw = nw_ref[...]
        for sl in range(acc.shape[1] // NORM_SLAB):
            a = acc[:, sl * NORM_SLAB:(sl + 1) * NORM_SLAB]
            ss = jnp.dot((a * a).astype(BF16), ones_bd, preferred_element_type=F32)
            out = a * lax.rsqrt(ss * (1.0 / ATT_QK_DIM) + EPS) * nw[:, sl * NORM_SLAB:(sl + 1) * NORM_SLAB]
            o_ref[:, sl * NORM_SLAB:(sl + 1) * NORM_SLAB] = out.astype(o_ref.dtype)


def _att_in_proj(u, w, layer, qk_w, n_qk, tm, tn):
    s, d = u.shape
    n_out = w.shape[2]
    n_qk_tiles = n_qk // tn
    kernel = functools.partial(_att_in_kernel, n_qk_tiles=n_qk_tiles)
    return pl.pallas_call(
        kernel,
        out_shape=jax.ShapeDtypeStruct((s, n_out), BF16),
        grid=(n_out // tn, s // tm),
        in_specs=[pl.BlockSpec((tm, d), lambda n, m: (m, 0)),
                  _weight_tile_spec(layer, d, tn, 0),
                  pl.BlockSpec((1, tn), lambda n, m: (0, jnp.minimum(n, n_qk_tiles - 1)))],
        out_specs=pl.BlockSpec((tm, tn), lambda n, m: (m, n)),
        scratch_shapes=[pltpu.VMEM((d, tn), BF16)],
        compiler_params=_params("arbitrary", "arbitrary"),
        name="att_in_proj",
    )(u, w, qk_w)


def _bias_kernel(rel_ref, far_ref, bk_ref, o_ref):
    h = pl.program_id(0)
    bk = bk_ref[...]
    far = far_ref[h]
    acc = jnp.zeros(bk.shape, F32)
    for b in range(NUM_BUCKETS):
        acc = jnp.where(bk == b, (rel_ref[b, h] - far) * LOG2E, acc)
    o_ref[0] = jnp.where(bk < 0, MASK_VALUE, acc)


def _bias_tiles(rel_bias, far_bias, buckets):
    nb, nh = rel_bias.shape
    t, t2 = buckets.shape
    return pl.pallas_call(
        _bias_kernel,
        out_shape=jax.ShapeDtypeStruct((nh, t, t2), F32),
        grid=(nh,),
        in_specs=[pl.BlockSpec(memory_space=pltpu.SMEM),
                  pl.BlockSpec(memory_space=pltpu.SMEM),
                  pl.BlockSpec((t, t2), lambda h: (0, 0))],
        out_specs=pl.BlockSpec((1, t, t2), lambda h: (h, 0, 0)),
        compiler_params=_params("arbitrary"),
        name="att_bias_tiles",
    )(rel_bias, far_bias, buckets)


def _t5_bucket(rel):
    nb = NUM_BUCKETS // 2
    ret = (rel > 0).astype(jnp.int32) * nb
    n = jnp.abs(rel)
    max_exact = nb // 2
    nf = jnp.maximum(n, max_exact).astype(F32)
    large = max_exact + (jnp.log(nf / max_exact) / math.log(MAX_DISTANCE / max_exact)
                         * (nb - max_exact)).astype(jnp.int32)
    large = jnp.minimum(large, nb - 1)
    return ret + jnp.where(n < max_exact, n, large)


def _stack_maps(q):
    lo = lax.broadcasted_iota(jnp.int32, (1, LANES), 1) < ATT_QK_DIM
    zero16 = jnp.zeros((), BF16)
    return jnp.concatenate([jnp.where(lo, q, zero16), jnp.where(lo, zero16, q)], axis=0)


def _attn_finish(acc, l, lam, sw, out_scale, t, o_ref):
    o = acc / l
    o = o[:t] - lam * o[t:]
    o_ref[...] = (_rms_rows(o, sw) * out_scale).astype(o_ref.dtype)


def _attn_fixed_kernel(lam_ref, shift_ref, q_ref, k_ref, v_ref, bias_ref, sw_ref, o_ref, l_ref, acc_ref,
                       *, tile, far_widths, out_scale):
    shift = shift_ref[pl.program_id(0)]
    qi = pl.program_id(1)
    t = tile
    qq = _stack_maps(q_ref[...])
    l_ref[...] = jnp.zeros_like(l_ref)
    acc_ref[...] = jnp.zeros_like(acc_ref)

    def step(first_tile, n_tiles, bias):
        width = n_tiles * t
        start = pl.multiple_of(first_tile * t, t)
        k = k_ref[pl.ds(start, width), :]
        v = v_ref[pl.ds(start, width), :]
        s = lax.dot_general(qq, k, (((1,), (1,)), ((), ())), preferred_element_type=F32)
        if bias is None:
            p = jnp.exp2(s - shift)
        else:
            b = bias - shift
            p = jnp.exp2(s + jnp.concatenate([b, b], axis=0))
        part = p[:, :LANES]
        for c in range(1, width // LANES):
            part = part + p[:, c * LANES:(c + 1) * LANES]
        l_ref[...] += part
        acc_ref[...] += jnp.dot(p.astype(BF16), v, preferred_element_type=F32)

    n_far = jnp.maximum(qi - 1, 0)
    w0 = far_widths[0]

    def wide_body(i, carry):
        step(i * w0, w0, None)
        return carry

    lax.fori_loop(0, n_far // w0, wide_body, 0)
    for w in far_widths[1:]:
        @pl.when((n_far & w) != 0)
        def _(w=w):
            step(n_far & ~(2 * w - 1), w, None)

    @pl.when(qi > 0)
    def _():
        step(qi - 1, 2, bias_ref[0])

    @pl.when(qi == 0)
    def _():
        step(0, 1, bias_ref[0, :, t:])

    l = jnp.sum(l_ref[...], axis=-1, keepdims=True)
    _attn_finish(acc_ref[...], l, lam_ref[0], sw_ref[...], out_scale, t, o_ref)


def _attn_online_kernel(lam_ref, q_ref, k_ref, v_ref, bias_ref, sw_ref, o_ref, m_ref, l_ref, acc_ref,
                        *, tile, out_scale):
    qi = pl.program_id(1)
    t = tile
    qq = _stack_maps(q_ref[...])
    m_ref[...] = jnp.full_like(m_ref, MASK_VALUE)
    l_ref[...] = jnp.zeros_like(l_ref)
    acc_ref[...] = jnp.zeros_like(acc_ref)

    def step(kt, bias):
        start = pl.multiple_of(kt * t, t)
        k = k_ref[pl.ds(start, t), :]
        v = v_ref[pl.ds(start, t), :]
        s = lax.dot_general(qq, k, (((1,), (1,)), ((), ())), preferred_element_type=F32)
        if bias is not None:
            s = s + jnp.concatenate([bias, bias], axis=0)
        m_old = m_ref[...]
        m_new = jnp.maximum(m_old, jnp.max(s, axis=-1, keepdims=True))
        alpha = jnp.exp2(m_old - m_new)
        p = jnp.exp2(s - m_new)
        l_ref[...] = alpha * l_ref[...] + jnp.sum(p, axis=-1, keepdims=True)
        acc_ref[...] = alpha * acc_ref[...] + jnp.dot(p.astype(BF16), v, preferred_element_type=F32)
        m_ref[...] = m_new

    def far_body(kt, carry):
        step(kt, None)
        return carry

    lax.fori_loop(0, jnp.maximum(qi - 1, 0), far_body, 0)

    @pl.when(qi > 0)
    def _():
        step(qi - 1, bias_ref[0, :, :t])

    step(qi, bias_ref[0, :, t:])
    _attn_finish(acc_ref[...], l_ref[...], lam_ref[0], sw_ref[...], out_scale, t, o_ref)


def _attention(lam, shift, qkv, bias, subln_w, n_heads, tile, out_scale):
    s = qkv.shape[0]
    smem = pl.BlockSpec(memory_space=pltpu.SMEM)
    in_specs = [pl.BlockSpec((tile, LANES), lambda h, i: (i, h)),
                pl.BlockSpec((s, LANES), lambda h, i: (0, n_heads + h)),
                pl.BlockSpec((s, LANES), lambda h, i: (0, 2 * n_heads + h)),
                pl.BlockSpec((1, tile, 2 * tile), lambda h, i: (h, 0, 0)),
                pl.BlockSpec((1, ATT_V_DIM), lambda h, i: (0, 0))]
    args = (qkv, qkv, qkv, bias, subln_w.reshape(1, ATT_V_DIM))
    acc = pltpu.VMEM((2 * tile, ATT_V_DIM), F32)
    if shift is not None:
        kernel = functools.partial(_attn_fixed_kernel, tile=tile, far_widths=ATT_FAR_WIDTHS, out_scale=out_scale)
        in_specs = [smem, smem] + in_specs
        args = (lam, shift) + args
        scratch = [pltpu.VMEM((2 * tile, LANES), F32), acc]
        name = "diff_attention"
    else:
        kernel = functools.partial(_attn_online_kernel, tile=tile, out_scale=out_scale)
        in_specs = [smem] + in_specs
        args = (lam,) + args
        scratch = [pltpu.VMEM((2 * tile, 1), F32), pltpu.VMEM((2 * tile, 1), F32), acc]
        name = "diff_attention_online"
    return pl.pallas_call(
        kernel,
        out_shape=jax.ShapeDtypeStruct((s, n_heads * ATT_V_DIM), BF16),
        grid=(n_heads, s // tile),
        in_specs=in_specs,
        out_specs=pl.BlockSpec((tile, ATT_V_DIM), lambda h, i: (i, h)),
        scratch_shapes=scratch,
        compiler_params=_params("arbitrary", "arbitrary"),
        name=name,
    )(*args)


def kernel(x, norm_mix_w, norm_mlp_w, ssm_in_w, ssm_conv_w, ssm_conv_b, ssm_dt_bias, ssm_a_log, ssm_d, ssm_norm_w, ssm_out_w, att_in_w, att_q_norm_w, att_k_norm_w, att_lam_q1, att_lam_k1, att_lam_q2, att_lam_k2, att_subln_w, att_out_w, rel_bias, mlp_w1, mlp_w2):
    bsz, s, d = x.shape
    assert bsz == 1
    depth = norm_mix_w.shape[0]
    d_inner = ssm_out_w.shape[1]
    n_ssm_heads = ssm_dt_bias.shape[1]
    conv_dim = ssm_conv_w.shape[2]
    n_att_heads = rel_bias.shape[1]
    r = n_ssm_heads // SSM_GROUPS

    tm = min(ROW_TILE, s)
    tmr = min(RES_ROW_TILE, s)
    chunk = min(SSD_CHUNK, s)
    tile = min(ATT_TILE, s)
    assert s % tm == 0 and s % tmr == 0 and s % chunk == 0 and s % tile == 0
    assert tile % ATT_CHUNK == 0 and tile > 90

    h = x.reshape(s, d)
    u = _rmsnorm(h, norm_mix_w[0], tmr)

    ti = jnp.arange(tile, dtype=jnp.int32)[:, None]
    tj = jnp.arange(2 * tile, dtype=jnp.int32)[None, :] - tile
    allowed = (tj < 0) | ((tj // ATT_CHUNK) <= (ti // ATT_CHUNK))
    buckets = jnp.where(allowed, _t5_bucket(tj - ti), -1)
    far_bias = rel_bias[_t5_bucket(jnp.int32(-2 * tile))]
    bias = _bias_tiles(rel_bias, far_bias, buckets)
    near_bias = rel_bias.astype(F32) - far_bias[None, :]
    near_hi = jnp.maximum(jnp.max(near_bias, axis=0), 0.0)
    near_lo = jnp.minimum(jnp.min(near_bias, axis=0), 0.0)

    ssm_out_wb, att_out_wb = ssm_out_w.astype(BF16), att_out_w.astype(BF16)
    mlp_w1b, mlp_w2b = mlp_w1.astype(BF16), mlp_w2.astype(BF16)

    for i in range(depth):
        j = i // 2
        if i % 2 == 0:
            w_dt = ssm_in_w[j, :, d_inner + conv_dim:].astype(BF16)
            z = _proj(u, ssm_in_w, j, 0, d_inner, tm, COL_TILE)
            xbc = _ssm_xbc_proj(u, ssm_in_w, j, 0.5 * ssm_conv_w[j], 0.5 * ssm_conv_b[j].reshape(1, conv_dim),
                                d_inner, tm, COL_TILE)
            a = -jnp.exp(ssm_a_log[j].astype(F32))
            mult = jnp.concatenate([jnp.ones_like(a), a * LOG2E]).reshape(1, -1)
            dt_b = jnp.tile(ssm_dt_bias[j].astype(F32), 2).reshape(1, -1)
            nar = _dt_proj(u, jnp.concatenate([w_dt, w_dt], axis=1), dt_b, mult, chunk, tm)
            cols = nar.reshape(s, 2, SSM_GROUPS, r).transpose(2, 0, 1, 3).reshape(SSM_GROUPS, s, 2 * r)
            rows = cols.transpose(0, 2, 1)
            d_exp = jnp.repeat(ssm_d[j].astype(F32), SSM_HEAD_DIM).reshape(1, d_inner)
            y = _ssd(z, xbc, cols, rows, d_exp, ssm_norm_w[j].reshape(1, d_inner), d_inner, chunk)
            h, u = _proj_res(y, ssm_out_wb, j, h, norm_mlp_w[i], tmr, d_inner // 2)
        else:
            lambda_init = 0.8 - 0.6 * math.exp(-0.3 * i)
            lam = (jnp.exp(jnp.sum(att_lam_q1[j].astype(F32) * att_lam_k1[j].astype(F32)))
                   - jnp.exp(jnp.sum(att_lam_q2[j].astype(F32) * att_lam_k2[j].astype(F32)))
                   + lambda_init).reshape(1)
            reps = d // ATT_QK_DIM
            wq = att_q_norm_w[j].astype(F32)
            wk = att_k_norm_w[j].astype(F32)
            qk_w = jnp.concatenate([jnp.tile(wq, reps) * (ATT_QK_DIM ** -0.5 * LOG2E),
                                    jnp.tile(wk, reps)]).reshape(1, 2 * d)
            qkv = _att_in_proj(u, att_in_w, j, qk_w, 2 * d, tm, COL_TILE)
            logit_bound = (ATT_QK_DIM ** 0.5) * jnp.max(jnp.abs(wq * wk))
            shift = (logit_bound + near_hi) * LOG2E
            spread = jnp.max((2.0 * logit_bound + near_hi - near_lo) * LOG2E)
            att_args = (qkv, bias, att_subln_w[j], n_att_heads, tile, 1.0 - lambda_init)
            o = lax.cond(spread <= EXP2_SAFE_RANGE,
                         lambda: _attention(lam, shift, *att_args),
                         lambda: _attention(lam, None, *att_args))
            h, u = _proj_res(o, att_out_wb, j, h, norm_mlp_w[i], tmr, d)
        nw_next = norm_mix_w[i + 1] if i + 1 < depth else None
        h, u = _mlp(u, mlp_w1b, mlp_w2b, i, h, nw_next, tm, MLP_HIDDEN_TILE)
    return h.reshape(bsz, s, d)
```

```python
import functools
import math

import jax
import jax.numpy as jnp
from jax import lax
from jax.experimental import pallas as pl
from jax.experimental.pallas import tpu as pltpu

F32 = jnp.float32
BF16 = jnp.bfloat16

EPS = 1e-6
MASK_VALUE = -1e30

SSM_HEAD_DIM = 64
SSM_GROUPS = 8
SSM_STATE = 128
SSM_CONV = 4
ATT_QK_DIM = 64
ATT_V_DIM = 128
ATT_CHUNK = 64
NUM_BUCKETS = 32
MAX_DISTANCE = 128

LANES = 128
SUBLANES = 8
MXU_COLS = 512
VMEM_LIMIT_BYTES = 56 * 1024 * 1024

ROW_TILE = 1024
COL_TILE = 1024
RES_ROW_TILE = 512
MLP_HIDDEN_TILE = 512
SSD_CHUNK = 256
ATT_TILE = 512
ATT_FAR_WIDTHS = (4, 2, 1)

LOG2E = 1.4426950408889634
EXP2_SAFE_RANGE = 120.0
NORM_SLAB = 256


def _params(*sem):
    return pltpu.CompilerParams(dimension_semantics=sem, vmem_limit_bytes=VMEM_LIMIT_BYTES)


def _silu_from_half(h):
    return h + h * jnp.tanh(h)


def _rms_rows(x, w):
    ms = jnp.mean(x * x, axis=-1, keepdims=True)
    return x * lax.rsqrt(ms + EPS) * w


def _rmsnorm_kernel(x_ref, w_ref, o_ref):
    o_ref[...] = _rms_rows(x_ref[...], w_ref[...]).astype(o_ref.dtype)


def _rmsnorm(x, w, tm):
    s, d = x.shape
    return pl.pallas_call(
        _rmsnorm_kernel,
        out_shape=jax.ShapeDtypeStruct((s, d), BF16),
        grid=(s // tm,),
        in_specs=[pl.BlockSpec((tm, d), lambda i: (i, 0)),
                  pl.BlockSpec((1, d), lambda i: (0, 0))],
        out_specs=pl.BlockSpec((tm, d), lambda i: (i, 0)),
        compiler_params=_params("arbitrary"),
        name="rmsnorm_in",
    )(x, w.reshape(1, d))


def _bf16_weight_tile(w_ref, wb_ref=None):
    if w_ref.dtype == BF16:
        return w_ref

    @pl.when(pl.program_id(1) == 0)
    def _():
        wb_ref[...] = w_ref[...].astype(BF16)
    return wb_ref


def _weight_scratch(w, d, tn):
    return [] if w.dtype == BF16 else [pltpu.VMEM((d, tn), BF16)]


def _weight_tile_spec(layer, d, tn, c0):
    return pl.BlockSpec((None, d, tn), lambda n, m: (layer, 0, c0 + n))


def _proj_kernel(u_ref, w_ref, o_ref, *wb_ref):
    wb_ref = _bf16_weight_tile(w_ref, *wb_ref)
    o_ref[...] = jnp.dot(u_ref[...], wb_ref[...], preferred_element_type=F32).astype(o_ref.dtype)


def _proj(u, w, layer, col0, n_out, tm, tn):
    s, d = u.shape
    return pl.pallas_call(
        _proj_kernel,
        out_shape=jax.ShapeDtypeStruct((s, n_out), BF16),
        grid=(n_out // tn, s // tm),
        in_specs=[pl.BlockSpec((tm, d), lambda n, m: (m, 0)),
                  _weight_tile_spec(layer, d, tn, col0 // tn)],
        out_specs=pl.BlockSpec((tm, tn), lambda n, m: (m, n)),
        scratch_shapes=_weight_scratch(w, d, tn),
        compiler_params=_params("arbitrary", "arbitrary"),
        name="ssm_z_proj",
    )(u, w)


def _ssm_xbc_kernel(u_ref, w_ref, cw_ref, cb_ref, o_ref, carry_ref, *wb_ref):
    tm = u_ref.shape[0]
    groups = tm // SUBLANES
    wb_ref = _bf16_weight_tile(w_ref, *wb_ref)

    @pl.when(pl.program_id(1) == 0)
    def _():
        carry_ref[...] = jnp.zeros_like(carry_ref)

    u = u_ref[...]
    sub = lax.broadcasted_iota(jnp.int32, (1, SUBLANES, MXU_COLS), 1)
    for c in range(o_ref.shape[1] // MXU_COLS):
        cols = slice(c * MXU_COLS, (c + 1) * MXU_COLS)
        acc = jnp.dot(u, wb_ref[:, cols], preferred_element_type=F32)
        cw = cw_ref[:, cols]
        a3 = jnp.concatenate([carry_ref[:, cols], acc], axis=0).reshape(groups + 1, SUBLANES, MXU_COLS)
        half = a3[1:] * cw[SSM_CONV - 1:SSM_CONV, :] + cb_ref[:, cols]
        rot = a3
        for j in range(1, SSM_CONV):
            rot = pltpu.roll(rot, 1, axis=1)
            shifted = jnp.where(sub >= j, rot[1:], rot[:-1])
            half = half + shifted * cw[SSM_CONV - 1 - j:SSM_CONV - j, :]
        o_ref[:, cols] = _silu_from_half(half).reshape(tm, MXU_COLS).astype(o_ref.dtype)
        carry_ref[:, cols] = acc[tm - SUBLANES:, :]


def _ssm_xbc_proj(u, w, layer, conv_w, conv_b, col0, tm, tn):
    s, d = u.shape
    n_out = conv_w.shape[1]
    return pl.pallas_call(
        _ssm_xbc_kernel,
        out_shape=jax.ShapeDtypeStruct((s, n_out), BF16),
        grid=(n_out // tn, s // tm),
        in_specs=[pl.BlockSpec((tm, d), lambda n, m: (m, 0)),
                  _weight_tile_spec(layer, d, tn, col0 // tn),
                  pl.BlockSpec((SSM_CONV, tn), lambda n, m: (0, n)),
                  pl.BlockSpec((1, tn), lambda n, m: (0, n))],
        out_specs=pl.BlockSpec((tm, tn), lambda n, m: (m, n)),
        scratch_shapes=[pltpu.VMEM((SUBLANES, tn), F32)] + _weight_scratch(w, d, tn),
        compiler_params=_params("arbitrary", "arbitrary"),
        name="ssm_xbc_proj",
    )(u, w, conv_w, conv_b)


def _dt_kernel(u_ref, w_ref, b_ref, mult_ref, o_ref, *, chunk, n_heads):
    x = jnp.dot(u_ref[...], w_ref[...].astype(BF16), preferred_element_type=F32) + b_ref[...]
    sp = jnp.maximum(x, 0.0) + jnp.log1p(jnp.exp(-jnp.abs(x)))
    v = sp * mult_ref[...]
    ri = lax.broadcasted_iota(jnp.int32, (chunk, chunk), 0)
    ci = lax.broadcasted_iota(jnp.int32, (chunk, chunk), 1)
    tri = (ri >= ci).astype(F32)
    is_da = lax.broadcasted_iota(jnp.int32, (1, v.shape[1]), 1) >= n_heads
    for c in range(v.shape[0] // chunk):
        blk = v[c * chunk:(c + 1) * chunk]
        cs = jnp.dot(tri, blk, precision=lax.Precision.HIGHEST, preferred_element_type=F32)
        o_ref[c * chunk:(c + 1) * chunk, :] = jnp.where(is_da, cs, blk)


def _dt_proj(u, w2, b2, mult, chunk, tm):
    s, d = u.shape
    n = w2.shape[1]
    kernel = functools.partial(_dt_kernel, chunk=chunk, n_heads=n // 2)
    return pl.pallas_call(
        kernel,
        out_shape=jax.ShapeDtypeStruct((s, n), F32),
        grid=(s // tm,),
        in_specs=[pl.BlockSpec((tm, d), lambda i: (i, 0)),
                  pl.BlockSpec((d, n), lambda i: (0, 0)),
                  pl.BlockSpec((1, n), lambda i: (0, 0)),
                  pl.BlockSpec((1, n), lambda i: (0, 0))],
        out_specs=pl.BlockSpec((tm, n), lambda i: (i, 0)),
        compiler_params=_params("arbitrary"),
        name="ssm_dt_proj",
    )(u, w2, b2, mult)


def _ssd_kernel(z_ref, x_ref, b_ref, c_ref, cols_ref, rows_ref, d_ref, nw_ref, o_ref, st_ref,
                *, chunk, heads_per_group):
    @pl.when(pl.program_id(1) == 0)
    def _():
        st_ref[...] = jnp.zeros_like(st_ref)

    r = heads_per_group
    x = x_ref[...]
    bm = b_ref[...]
    cm = c_ref[...]
    bm32 = bm.astype(F32)
    cm32 = cm.astype(F32)
    cols = cols_ref[0]
    rows = rows_ref[0]
    cb = lax.dot_general(cm, bm, (((1,), (1,)), ((), ())), preferred_element_type=F32)
    hc = chunk // 2
    ri = lax.broadcasted_iota(jnp.int32, (hc, hc), 0)
    ci = lax.broadcasted_iota(jnp.int32, (hc, hc), 1)
    causal = ri >= ci
    cb_tt, cb_bt, cb_bb = cb[:hc, :hc], cb[hc:, :hc], cb[hc:, hc:]
    src_rows = rows[r:, :] - jnp.log2(rows[:r, :])
    wt_cols = jnp.log2(cols[:, :r]) - cols[:, r:]
    lo = lax.broadcasted_iota(jnp.int32, (1, LANES), 1) < SSM_HEAD_DIM
    zero16 = jnp.zeros((), BF16)

    ys = []
    for p in range(r // 2):
        m_top, m_bot, cs, bw, dec = [], [], [], [], []
        for jj in range(2):
            j = 2 * p + jj
            a_b = jnp.broadcast_to(cols[:, r + j:r + j + 1], (chunk, LANES))
            w_b = jnp.broadcast_to(wt_cols[:, j:j + 1], (chunk, LANES))
            s_row = src_rows[j:j + 1, :]
            d_tt = jnp.exp2(jnp.where(causal, a_b[:hc] - s_row[:, :hc], MASK_VALUE))
            d_bt = jnp.exp2(a_b[hc:] - s_row[:, :hc])
            d_bb = jnp.exp2(jnp.where(causal, a_b[hc:] - s_row[:, hc:], MASK_VALUE))
            m_top.append((cb_tt * d_tt).astype(BF16))
            m_bot.append(jnp.concatenate([(cb_bt * d_bt).astype(BF16), (cb_bb * d_bb).astype(BF16)], axis=1))
            cs.append((cm32 * jnp.exp2(a_b)).astype(BF16))
            a_last = a_b[chunk - 1:chunk, :]
            bw.append((bm32 * jnp.exp2(a_last + w_b)).astype(BF16))
            dec.append(a_last)
        xp = x[:, p * LANES:(p + 1) * LANES]
        xlo, xhi = jnp.where(lo, xp, zero16), jnp.where(lo, zero16, xp)
        xbd = jnp.concatenate([xlo, xhi], axis=0)
        st = st_ref[:, p * LANES:(p + 1) * LANES]
        stb = st.astype(BF16)
        stbd = jnp.concatenate([jnp.where(lo, stb, zero16), jnp.where(lo, zero16, stb)], axis=0)
        y_top = jnp.dot(jnp.concatenate(m_top, axis=1), jnp.concatenate([xlo[:hc], xhi[:hc]], axis=0),
                        preferred_element_type=F32)
        y_bot = jnp.dot(jnp.concatenate(m_bot, axis=1), xbd, preferred_element_type=F32)
        y = (jnp.concatenate([y_top, y_bot], axis=0)
             + jnp.dot(jnp.concatenate(cs, axis=1), stbd, preferred_element_type=F32))
        upd = lax.dot_general(jnp.concatenate(bw, axis=0), xbd, (((0,), (0,)), ((), ())),
                              preferred_element_type=F32)
        st_ref[:, p * LANES:(p + 1) * LANES] = st * jnp.exp2(jnp.where(lo, dec[0], dec[1])) + upd
        ys.append(y)

    y = jnp.concatenate(ys, axis=1) + x.astype(F32) * d_ref[...]
    z = z_ref[...].astype(F32)
    gated = y * _silu_from_half(0.5 * z)
    o_ref[...] = _rms_rows(gated, nw_ref[...]).astype(o_ref.dtype)


def _ssd(z, xbc, cols, rows, d_exp, norm_w, d_inner, chunk):
    s = z.shape[0]
    g = SSM_GROUPS
    gw = d_inner // g
    r = gw // SSM_HEAD_DIM
    n = SSM_STATE
    b_off = d_inner // n
    c_off = b_off + g
    kernel = functools.partial(_ssd_kernel, chunk=chunk, heads_per_group=r)
    return pl.pallas_call(
        kernel,
        out_shape=jax.ShapeDtypeStruct((s, d_inner), BF16),
        grid=(g, s // chunk),
        in_specs=[pl.BlockSpec((chunk, gw), lambda gi, c: (c, gi)),
                  pl.BlockSpec((chunk, gw), lambda gi, c: (c, gi)),
                  pl.BlockSpec((chunk, n), lambda gi, c: (c, b_off + gi)),
                  pl.BlockSpec((chunk, n), lambda gi, c: (c, c_off + gi)),
                  pl.BlockSpec((1, chunk, 2 * r), lambda gi, c: (gi, c, 0)),
                  pl.BlockSpec((1, 2 * r, chunk), lambda gi, c: (gi, 0, c)),
                  pl.BlockSpec((1, gw), lambda gi, c: (0, gi)),
                  pl.BlockSpec((1, gw), lambda gi, c: (0, gi))],
        out_specs=pl.BlockSpec((chunk, gw), lambda gi, c: (c, gi)),
        scratch_shapes=[pltpu.VMEM((n, gw), F32)],
        compiler_params=_params("arbitrary", "arbitrary"),
        name="ssd_scan",
    )(z, xbc, xbc, xbc, cols, rows, d_exp, norm_w)


def _proj_res_kernel(y_ref, w_ref, h_ref, nw_ref, ho_ref, uo_ref):
    hn = h_ref[...] + jnp.dot(y_ref[...], w_ref[...], preferred_element_type=F32)
    ho_ref[...] = hn
    uo_ref[...] = _rms_rows(hn, nw_ref[...]).astype(uo_ref.dtype)


def _proj_res(y, w, layer, h, norm_w, tm):
    s, kd = y.shape
    d = w.shape[2]
    row_spec = pl.BlockSpec((tm, d), lambda m: (m, 0))
    return pl.pallas_call(
        _proj_res_kernel,
        out_shape=(jax.ShapeDtypeStruct((s, d), F32), jax.ShapeDtypeStruct((s, d), BF16)),
        grid=(s // tm,),
        in_specs=[pl.BlockSpec((tm, kd), lambda m: (m, 0)),
                  pl.BlockSpec((None, kd, d), lambda m: (layer, 0, 0), pipeline_mode=pl.Buffered(1)),
                  row_spec,
                  pl.BlockSpec((1, d), lambda m: (0, 0))],
        out_specs=(row_spec, row_spec),
        compiler_params=_params("arbitrary"),
        name="proj_residual",
    )(y, w, h, norm_w.reshape(1, d))


def _mlp_kernel(u_ref, w1_ref, w2_ref, h_ref, *rest, emit_norm):
    if emit_norm:
        nw_ref, ho_ref, uo_ref, acc_ref = rest
    else:
        ho_ref, acc_ref = rest
    j = pl.program_id(1)

    @pl.when(j == 0)
    def _():
        acc_ref[...] = jnp.zeros_like(acc_ref)

    a = jnp.dot(u_ref[...], w1_ref[...], preferred_element_type=F32)
    a = jnp.square(jnp.maximum(a, 0.0)).astype(BF16)
    acc_ref[...] += jnp.dot(a, w2_ref[...], preferred_element_type=F32)

    @pl.when(j == pl.num_programs(1) - 1)
    def _():
        hn = h_ref[...] + acc_ref[...]
        ho_ref[...] = hn
        if emit_norm:
            uo_ref[...] = _rms_rows(hn, nw_ref[...]).astype(uo_ref.dtype)


def _mlp(u, w1, w2, layer, h, norm_w, tm, th):
    s, d = u.shape
    hd = w1.shape[2]
    emit_norm = norm_w is not None
    row_spec = pl.BlockSpec((tm, d), lambda m, j: (m, 0))
    once_spec = pl.BlockSpec((tm, d), lambda m, j: (m, 0), pipeline_mode=pl.Buffered(1))
    in_specs = [row_spec,
                pl.BlockSpec((None, d, th), lambda m, j: (layer, 0, j)),
                pl.BlockSpec((None, th, d), lambda m, j: (layer, j, 0)),
                once_spec]
    args = [u, w1, w2, h]
    out_shape = [jax.ShapeDtypeStruct((s, d), F32)]
    out_specs = [once_spec]
    if emit_norm:
        in_specs.append(pl.BlockSpec((1, d), lambda m, j: (0, 0)))
        args.append(norm_w.reshape(1, d))
        out_shape.append(jax.ShapeDtypeStruct((s, d), BF16))
        out_specs.append(once_spec)
    out = pl.pallas_call(
        functools.partial(_mlp_kernel, emit_norm=emit_norm),
        out_shape=tuple(out_shape),
        grid=(s // tm, hd // th),
        in_specs=in_specs,
        out_specs=tuple(out_specs),
        scratch_shapes=[pltpu.VMEM((tm, d), F32)],
        compiler_params=_params("arbitrary", "arbitrary"),
        name="mlp_residual",
    )(*args)
    return out if emit_norm else (out[0], None)


def _att_in_kernel(u_ref, w_ref, nw_ref, o_ref, *wb_ref, n_qk_tiles):
    n = pl.program_id(0)
    wb_ref = _bf16_weight_tile(w_ref, *wb_ref)
    acc = jnp.dot(u_ref[...], wb_ref[...], preferred_element_type=F32)

    @pl.when(n >= n_qk_tiles)
    def _():
        o_ref[...] = acc.astype(o_ref.dtype)

    @pl.when(n < n_qk_tiles)
    def _():
        gi = lax.broadcasted_iota(jnp.int32, (NORM_SLAB, NORM_SLAB), 0) // ATT_QK_DIM
        gj = lax.broadcasted_iota(jnp.int32, (NORM_SLAB, NORM_SLAB), 1) // ATT_QK_DIM
        ones_bd = (gi == gj).astype(BF16)
        nw = nw_ref[...]
        for sl in range(acc.shape[1] // NORM_SLAB):
            a = acc[:, sl * NORM_SLAB:(sl + 1) * NORM_SLAB]
            ss = jnp.dot((a * a).astype(BF16), ones_bd, preferred_element_type=F32)
            out = a * lax.rsqrt(ss * (1.0 / ATT_QK_DIM) + EPS) * nw[:, sl * NORM_SLAB:(sl + 1) * NORM_SLAB]
            o_ref[:, sl * NORM_SLAB:(sl + 1) * NORM_SLAB] = out.astype(o_ref.dtype)


def _att_in_proj(u, w, layer, qk_w, n_qk, tm, tn):
    s, d = u.shape
    n_out = w.shape[2]
    n_qk_tiles = n_qk // tn
    kernel = functools.partial(_att_in_kernel, n_qk_tiles=n_qk_tiles)
    return pl.pallas_call(
        kernel,
        out_shape=jax.ShapeDtypeStruct((s, n_out), BF16),
        grid=(n_out // tn, s // tm),
        in_specs=[pl.BlockSpec((tm, d), lambda n, m: (m, 0)),
                  _weight_tile_spec(layer, d, tn, 0),
                  pl.BlockSpec((1, tn), lambda n, m: (0, jnp.minimum(n, n_qk_tiles - 1)))],
        out_specs=pl.BlockSpec((tm, tn), lambda n, m: (m, n)),
        scratch_shapes=_weight_scratch(w, d, tn),
        compiler_params=_params("arbitrary", "arbitrary"),
        name="att_in_proj",
    )(u, w, qk_w)


def _bias_kernel(rel_ref, far_ref, bk_ref, o_ref):
    h = pl.program_id(0)
    bk = bk_ref[...]
    far = far_ref[h]
    acc = jnp.zeros(bk.shape, F32)
    for b in range(NUM_BUCKETS):
        acc = jnp.where(bk == b, (rel_ref[b, h] - far) * LOG2E, acc)
    o_ref[0] = jnp.where(bk < 0, MASK_VALUE, acc)


def _bias_tiles(rel_bias, far_bias, buckets):
    nb, nh = rel_bias.shape
    t, t2 = buckets.shape
    return pl.pallas_call(
        _bias_kernel,
        out_shape=jax.ShapeDtypeStruct((nh, t, t2), F32),
        grid=(nh,),
        in_specs=[pl.BlockSpec(memory_space=pltpu.SMEM),
                  pl.BlockSpec(memory_space=pltpu.SMEM),
                  pl.BlockSpec((t, t2), lambda h: (0, 0))],
        out_specs=pl.BlockSpec((1, t, t2), lambda h: (h, 0, 0)),
        compiler_params=_params("arbitrary"),
        name="att_bias_tiles",
    )(rel_bias, far_bias, buckets)


def _t5_bucket(rel):
    nb = NUM_BUCKETS // 2
    ret = (rel > 0).astype(jnp.int32) * nb
    n = jnp.abs(rel)
    max_exact = nb // 2
    nf = jnp.maximum(n, max_exact).astype(F32)
    large = max_exact + (jnp.log(nf / max_exact) / math.log(MAX_DISTANCE / max_exact)
                         * (nb - max_exact)).astype(jnp.int32)
    large = jnp.minimum(large, nb - 1)
    return ret + jnp.where(n < max_exact, n, large)


def _stack_maps(q):
    lo = lax.broadcasted_iota(jnp.int32, (1, LANES), 1) < ATT_QK_DIM
    zero16 = jnp.zeros((), BF16)
    return jnp.concatenate([jnp.where(lo, q, zero16), jnp.where(lo, zero16, q)], axis=0)


def _attn_finish(acc, l, lam, sw, out_scale, t, o_ref):
    o = acc / l
    o = o[:t] - lam * o[t:]
    o_ref[...] = (_rms_rows(o, sw) * out_scale).astype(o_ref.dtype)


def _attn_fixed_kernel(lam_ref, shift_ref, q_ref, k_ref, v_ref, bias_ref, sw_ref, o_ref, l_ref, acc_ref,
                       *, tile, far_widths, out_scale):
    shift = shift_ref[pl.program_id(0)]
    qi = pl.program_id(1)
    t = tile
    qq = _stack_maps(q_ref[...])
    l_ref[...] = jnp.zeros_like(l_ref)
    acc_ref[...] = jnp.zeros_like(acc_ref)

    def step(first_tile, n_tiles, bias):
        width = n_tiles * t
        start = pl.multiple_of(first_tile * t, t)
        k = k_ref[pl.ds(start, width), :]
        v = v_ref[pl.ds(start, width), :]
        s = lax.dot_general(qq, k, (((1,), (1,)), ((), ())), preferred_element_type=F32)
        if bias is None:
            p = jnp.exp2(s - shift)
        else:
            b = bias - shift
            p = jnp.exp2(s + jnp.concatenate([b, b], axis=0))
        part = p[:, :LANES]
        for c in range(1, width // LANES):
            part = part + p[:, c * LANES:(c + 1) * LANES]
        l_ref[...] += part
        acc_ref[...] += jnp.dot(p.astype(BF16), v, preferred_element_type=F32)

    n_far = jnp.maximum(qi - 1, 0)
    w0 = far_widths[0]

    def wide_body(i, carry):
        step(i * w0, w0, None)
        return carry

    lax.fori_loop(0, n_far // w0, wide_body, 0)
    for w in far_widths[1:]:
        @pl.when((n_far & w) != 0)
        def _(w=w):
            step(n_far & ~(2 * w - 1), w, None)

    @pl.when(qi > 0)
    def _():
        step(qi - 1, 2, bias_ref[0])

    @pl.when(qi == 0)
    def _():
        step(0, 1, bias_ref[0, :, t:])

    l = jnp.sum(l_ref[...], axis=-1, keepdims=True)
    _attn_finish(acc_ref[...], l, lam_ref[0], sw_ref[...], out_scale, t, o_ref)


def _attn_online_kernel(lam_ref, q_ref, k_ref, v_ref, bias_ref, sw_ref, o_ref, m_ref, l_ref, acc_ref,
                        *, tile, out_scale):
    qi = pl.program_id(1)
    t = tile
    qq = _stack_maps(q_ref[...])
    m_ref[...] = jnp.full_like(m_ref, MASK_VALUE)
    l_ref[...] = jnp.zeros_like(l_ref)
    acc_ref[...] = jnp.zeros_like(acc_ref)

    def step(kt, bias):
        start = pl.multiple_of(kt * t, t)
        k = k_ref[pl.ds(start, t), :]
        v = v_ref[pl.ds(start, t), :]
        s = lax.dot_general(qq, k, (((1,), (1,)), ((), ())), preferred_element_type=F32)
        if bias is not None:
            s = s + jnp.concatenate([bias, bias], axis=0)
        m_old = m_ref[...]
        m_new = jnp.maximum(m_old, jnp.max(s, axis=-1, keepdims=True))
        alpha = jnp.exp2(m_old - m_new)
        p = jnp.exp2(s - m_new)
        l_ref[...] = alpha * l_ref[...] + jnp.sum(p, axis=-1, keepdims=True)
        acc_ref[...] = alpha * acc_ref[...] + jnp.dot(p.astype(BF16), v, preferred_element_type=F32)
        m_ref[...] = m_new

    def far_body(kt, carry):
        step(kt, None)
        return carry

    lax.fori_loop(0, jnp.maximum(qi - 1, 0), far_body, 0)

    @pl.when(qi > 0)
    def _():
        step(qi - 1, bias_ref[0, :, :t])

    step(qi, bias_ref[0, :, t:])
    _attn_finish(acc_ref[...], l_ref[...], lam_ref[0], sw_ref[...], out_scale, t, o_ref)


def _attention(lam, shift, qkv, bias, subln_w, n_heads, tile, out_scale):
    s = qkv.shape[0]
    smem = pl.BlockSpec(memory_space=pltpu.SMEM)
    in_specs = [pl.BlockSpec((tile, LANES), lambda h, i: (i, h)),
                pl.BlockSpec((s, LANES), lambda h, i: (0, n_heads + h)),
                pl.BlockSpec((s, LANES), lambda h, i: (0, 2 * n_heads + h)),
                pl.BlockSpec((1, tile, 2 * tile), lambda h, i: (h, 0, 0)),
                pl.BlockSpec((1, ATT_V_DIM), lambda h, i: (0, 0))]
    args = (qkv, qkv, qkv, bias, subln_w.reshape(1, ATT_V_DIM))
    acc = pltpu.VMEM((2 * tile, ATT_V_DIM), F32)
    if shift is not None:
        kernel = functools.partial(_attn_fixed_kernel, tile=tile, far_widths=ATT_FAR_WIDTHS, out_scale=out_scale)
        in_specs = [smem, smem] + in_specs
        args = (lam, shift) + args
        scratch = [pltpu.VMEM((2 * tile, LANES), F32), acc]
        name = "diff_attention"
    else:
        kernel = functools.partial(_attn_online_kernel, tile=tile, out_scale=out_scale)
        in_specs = [smem] + in_specs
        args = (lam,) + args
        scratch = [pltpu.VMEM((2 * tile, 1), F32), pltpu.VMEM((2 * tile, 1), F32), acc]
        name = "diff_attention_online"
    return pl.pallas_call(
        kernel,
        out_shape=jax.ShapeDtypeStruct((s, n_heads * ATT_V_DIM), BF16),
        grid=(n_heads, s // tile),
        in_specs=in_specs,
        out_specs=pl.BlockSpec((tile, ATT_V_DIM), lambda h, i: (i, h)),
        scratch_shapes=scratch,
        compiler_params=_params("arbitrary", "arbitrary"),
        name=name,
    )(*args)


def kernel(x, norm_mix_w, norm_mlp_w, ssm_in_w, ssm_conv_w, ssm_conv_b, ssm_dt_bias, ssm_a_log, ssm_d, ssm_norm_w, ssm_out_w, att_in_w, att_q_norm_w, att_k_norm_w, att_lam_q1, att_lam_k1, att_lam_q2, att_lam_k2, att_subln_w, att_out_w, rel_bias, mlp_w1, mlp_w2):
    bsz, s, d = x.shape
    assert bsz == 1
    depth = norm_mix_w.shape[0]
    d_inner = ssm_out_w.shape[1]
    n_ssm_heads = ssm_dt_bias.shape[1]
    conv_dim = ssm_conv_w.shape[2]
    n_att_heads = rel_bias.shape[1]
    r = n_ssm_heads // SSM_GROUPS

    tm = min(ROW_TILE, s)
    tmr = min(RES_ROW_TILE, s)
    chunk = min(SSD_CHUNK, s)
    tile = min(ATT_TILE, s)
    assert s % tm == 0 and s % tmr == 0 and s % chunk == 0 and s % tile == 0
    assert tile % ATT_CHUNK == 0 and tile > 90

    h = x.reshape(s, d)
    u = _rmsnorm(h, norm_mix_w[0], tmr)

    ti = jnp.arange(tile, dtype=jnp.int32)[:, None]
    tj = jnp.arange(2 * tile, dtype=jnp.int32)[None, :] - tile
    allowed = (tj < 0) | ((tj // ATT_CHUNK) <= (ti // ATT_CHUNK))
    buckets = jnp.where(allowed, _t5_bucket(tj - ti), -1)
    far_bias = rel_bias[_t5_bucket(jnp.int32(-2 * tile))]
    bias = _bias_tiles(rel_bias, far_bias, buckets)
    near_bias = rel_bias.astype(F32) - far_bias[None, :]
    near_hi = jnp.maximum(jnp.max(near_bias, axis=0), 0.0)
    near_lo = jnp.minimum(jnp.min(near_bias, axis=0), 0.0)

    ssm_in_wb = ssm_in_w[:, :, :d_inner + conv_dim].astype(BF16)
    ssm_out_wb, att_out_wb = ssm_out_w.astype(BF16), att_out_w.astype(BF16)
    mlp_w1b, mlp_w2b = mlp_w1.astype(BF16), mlp_w2.astype(BF16)

    for i in range(depth):
        j = i // 2
        if i % 2 == 0:
            w_dt = ssm_in_w[j, :, d_inner + conv_dim:]
            z = _proj(u, ssm_in_wb, j, 0, d_inner, tm, COL_TILE)
            xbc = _ssm_xbc_proj(u, ssm_in_wb, j, 0.5 * ssm_conv_w[j], 0.5 * ssm_conv_b[j].reshape(1, conv_dim),
                                d_inner, tm, COL_TILE)
            a = -jnp.exp(ssm_a_log[j].astype(F32))
            mult = jnp.concatenate([jnp.ones_like(a), a * LOG2E]).reshape(1, -1)
            dt_b = jnp.tile(ssm_dt_bias[j].astype(F32), 2).reshape(1, -1)
            nar = _dt_proj(u, jnp.concatenate([w_dt, w_dt], axis=1), dt_b, mult, chunk, tm)
            cols = nar.reshape(s, 2, SSM_GROUPS, r).transpose(2, 0, 1, 3).reshape(SSM_GROUPS, s, 2 * r)
            rows = cols.transpose(0, 2, 1)
            d_exp = jnp.repeat(ssm_d[j].astype(F32), SSM_HEAD_DIM).reshape(1, d_inner)
            y = _ssd(z, xbc, cols, rows, d_exp, ssm_norm_w[j].reshape(1, d_inner), d_inner, chunk)
            h, u = _proj_res(y, ssm_out_wb, j, h, norm_mlp_w[i], tmr)
        else:
            lambda_init = 0.8 - 0.6 * math.exp(-0.3 * i)
            lam = (jnp.exp(jnp.sum(att_lam_q1[j].astype(F32) * att_lam_k1[j].astype(F32)))
                   - jnp.exp(jnp.sum(att_lam_q2[j].astype(F32) * att_lam_k2[j].astype(F32)))
                   + lambda_init).reshape(1)
            reps = d // ATT_QK_DIM
            wq = att_q_norm_w[j].astype(F32)
            wk = att_k_norm_w[j].astype(F32)
            qk_w = jnp.concatenate([jnp.tile(wq, reps) * (ATT_QK_DIM ** -0.5 * LOG2E),
                                    jnp.tile(wk, reps)]).reshape(1, 2 * d)
            qkv = _att_in_proj(u, att_in_w, j, qk_w, 2 * d, tm, COL_TILE)
            logit_bound = (ATT_QK_DIM ** 0.5) * jnp.max(jnp.abs(wq * wk))
            shift = (logit_bound + near_hi) * LOG2E
            spread = jnp.max((2.0 * logit_bound + near_hi - near_lo) * LOG2E)
            att_args = (qkv, bias, att_subln_w[j], n_att_heads, tile, 1.0 - lambda_init)
            o = lax.cond(spread <= EXP2_SAFE_RANGE,
                         lambda: _attention(lam, shift, *att_args),
                         lambda: _attention(lam, None, *att_args))
            h, u = _proj_res(o, att_out_wb, j, h, norm_mlp_w[i], tmr)
        nw_next = norm_mix_w[i + 1] if i + 1 < depth else None
        h, u = _mlp(u, mlp_w1b, mlp_w2b, i, h, nw_next, tm, MLP_HIDDEN_TILE)
    return h.reshape(bsz, s, d)
```

```python
import functools
import math

import jax
import jax.numpy as jnp
from jax import lax
from jax.experimental import pallas as pl
from jax.experimental.pallas import tpu as pltpu

F32 = jnp.float32
BF16 = jnp.bfloat16

EPS = 1e-6
MASK_VALUE = -1e30

SSM_HEAD_DIM = 64
SSM_GROUPS = 8
SSM_STATE = 128
SSM_CONV = 4
ATT_QK_DIM = 64
ATT_V_DIM = 128
ATT_CHUNK = 64
NUM_BUCKETS = 32
MAX_DISTANCE = 128

LANES = 128
SUBLANES = 8
MXU_COLS = 512
VMEM_LIMIT_BYTES = 56 * 1024 * 1024

ROW_TILE = 1024
COL_TILE = 1024
RES_ROW_TILE = 512
MLP_HIDDEN_TILE = 512
SSD_CHUNK = 256
ATT_TILE = 512
ATT_FAR_WIDTHS = (4, 2, 1)

LOG2E = 1.4426950408889634
EXP2_SAFE_RANGE = 120.0
NORM_SLAB = 256


def _params(*sem):
    return pltpu.CompilerParams(dimension_semantics=sem, vmem_limit_bytes=VMEM_LIMIT_BYTES)


def _silu_from_half(h):
    return h + h * jnp.tanh(h)


def _rms_rows(x, w):
    ms = jnp.mean(x * x, axis=-1, keepdims=True)
    return x * lax.rsqrt(ms + EPS) * w


def _rmsnorm_kernel(x_ref, w_ref, o_ref):
    o_ref[...] = _rms_rows(x_ref[...], w_ref[...]).astype(o_ref.dtype)


def _rmsnorm(x, w, tm):
    s, d = x.shape
    return pl.pallas_call(
        _rmsnorm_kernel,
        out_shape=jax.ShapeDtypeStruct((s, d), BF16),
        grid=(s // tm,),
        in_specs=[pl.BlockSpec((tm, d), lambda i: (i, 0)),
                  pl.BlockSpec((1, d), lambda i: (0, 0))],
        out_specs=pl.BlockSpec((tm, d), lambda i: (i, 0)),
        compiler_params=_params("arbitrary"),
        name="rmsnorm_in",
    )(x, w.reshape(1, d))


def _bf16_weight_tile(w_ref, wb_ref=None):
    if w_ref.dtype == BF16:
        return w_ref

    @pl.when(pl.program_id(1) == 0)
    def _():
        wb_ref[...] = w_ref[...].astype(BF16)
    return wb_ref


def _weight_scratch(w, d, tn):
    return [] if w.dtype == BF16 else [pltpu.VMEM((d, tn), BF16)]


def _weight_tile_spec(layer, d, tn, c0):
    return pl.BlockSpec((None, d, tn), lambda n, m: (layer, 0, c0 + n))


def _proj_kernel(u_ref, w_ref, o_ref, *wb_ref):
    wb_ref = _bf16_weight_tile(w_ref, *wb_ref)
    o_ref[...] = jnp.dot(u_ref[...], wb_ref[...], preferred_element_type=F32).astype(o_ref.dtype)


def _proj(u, w, layer, col0, n_out, tm, tn):
    s, d = u.shape
    return pl.pallas_call(
        _proj_kernel,
        out_shape=jax.ShapeDtypeStruct((s, n_out), BF16),
        grid=(n_out // tn, s // tm),
        in_specs=[pl.BlockSpec((tm, d), lambda n, m: (m, 0)),
                  _weight_tile_spec(layer, d, tn, col0 // tn)],
        out_specs=pl.BlockSpec((tm, tn), lambda n, m: (m, n)),
        scratch_shapes=_weight_scratch(w, d, tn),
        compiler_params=_params("arbitrary", "arbitrary"),
        name="ssm_z_proj",
    )(u, w)


def _ssm_xbc_kernel(u_ref, w_ref, cw_ref, cb_ref, o_ref, carry_ref, *wb_ref):
    tm = u_ref.shape[0]
    groups = tm // SUBLANES
    wb_ref = _bf16_weight_tile(w_ref, *wb_ref)

    @pl.when(pl.program_id(1) == 0)
    def _():
        carry_ref[...] = jnp.zeros_like(carry_ref)

    u = u_ref[...]
    sub = lax.broadcasted_iota(jnp.int32, (1, SUBLANES, MXU_COLS), 1)
    for c in range(o_ref.shape[1] // MXU_COLS):
        cols = slice(c * MXU_COLS, (c + 1) * MXU_COLS)
        acc = jnp.dot(u, wb_ref[:, cols], preferred_element_type=F32)
        cw = cw_ref[:, cols]
        a3 = jnp.concatenate([carry_ref[:, cols], acc], axis=0).reshape(groups + 1, SUBLANES, MXU_COLS)
        half = a3[1:] * cw[SSM_CONV - 1:SSM_CONV, :] + cb_ref[:, cols]
        rot = a3
        for j in range(1, SSM_CONV):
            rot = pltpu.roll(rot, 1, axis=1)
            shifted = jnp.where(sub >= j, rot[1:], rot[:-1])
            half = half + shifted * cw[SSM_CONV - 1 - j:SSM_CONV - j, :]
        o_ref[:, cols] = _silu_from_half(half).reshape(tm, MXU_COLS).astype(o_ref.dtype)
        carry_ref[:, cols] = acc[tm - SUBLANES:, :]


def _ssm_xbc_proj(u, w, layer, conv_w, conv_b, col0, tm, tn):
    s, d = u.shape
    n_out = conv_w.shape[1]
    return pl.pallas_call(
        _ssm_xbc_kernel,
        out_shape=jax.ShapeDtypeStruct((s, n_out), BF16),
        grid=(n_out // tn, s // tm),
        in_specs=[pl.BlockSpec((tm, d), lambda n, m: (m, 0)),
                  _weight_tile_spec(layer, d, tn, col0 // tn),
                  pl.BlockSpec((SSM_CONV, tn), lambda n, m: (0, n)),
                  pl.BlockSpec((1, tn), lambda n, m: (0, n))],
        out_specs=pl.BlockSpec((tm, tn), lambda n, m: (m, n)),
        scratch_shapes=[pltpu.VMEM((SUBLANES, tn), F32)] + _weight_scratch(w, d, tn),
        compiler_params=_params("arbitrary", "arbitrary"),
        name="ssm_xbc_proj",
    )(u, w, conv_w, conv_b)


def _dt_kernel(u_ref, w_ref, b_ref, mult_ref, o_ref, *, chunk, n_heads):
    x = jnp.dot(u_ref[...], w_ref[...].astype(BF16), preferred_element_type=F32) + b_ref[...]
    sp = jnp.maximum(x, 0.0) + jnp.log1p(jnp.exp(-jnp.abs(x)))
    v = sp * mult_ref[...]
    ri = lax.broadcasted_iota(jnp.int32, (chunk, chunk), 0)
    ci = lax.broadcasted_iota(jnp.int32, (chunk, chunk), 1)
    tri = (ri >= ci).astype(F32)
    is_da = lax.broadcasted_iota(jnp.int32, (1, v.shape[1]), 1) >= n_heads
    for c in range(v.shape[0] // chunk):
        blk = v[c * chunk:(c + 1) * chunk]
        cs = jnp.dot(tri, blk, precision=lax.Precision.HIGHEST, preferred_element_type=F32)
        o_ref[c * chunk:(c + 1) * chunk, :] = jnp.where(is_da, cs, blk)


def _dt_proj(u, w2, b2, mult, chunk, tm):
    s, d = u.shape
    n = w2.shape[1]
    kernel = functools.partial(_dt_kernel, chunk=chunk, n_heads=n // 2)
    return pl.pallas_call(
        kernel,
        out_shape=jax.ShapeDtypeStruct((s, n), F32),
        grid=(s // tm,),
        in_specs=[pl.BlockSpec((tm, d), lambda i: (i, 0)),
                  pl.BlockSpec((d, n), lambda i: (0, 0)),
                  pl.BlockSpec((1, n), lambda i: (0, 0)),
                  pl.BlockSpec((1, n), lambda i: (0, 0))],
        out_specs=pl.BlockSpec((tm, n), lambda i: (i, 0)),
        compiler_params=_params("arbitrary"),
        name="ssm_dt_proj",
    )(u, w2, b2, mult)


def _ssd_kernel(z_ref, x_ref, b_ref, c_ref, cols_ref, rows_ref, d_ref, nw_ref, o_ref, st_ref,
                *, chunk, heads_per_group):
    @pl.when(pl.program_id(1) == 0)
    def _():
        st_ref[...] = jnp.zeros_like(st_ref)

    r = heads_per_group
    x = x_ref[...]
    bm = b_ref[...]
    cm = c_ref[...]
    bm32 = bm.astype(F32)
    cm32 = cm.astype(F32)
    cols = cols_ref[0]
    rows = rows_ref[0]
    cb = lax.dot_general(cm, bm, (((1,), (1,)), ((), ())), preferred_element_type=F32)
    hc = chunk // 2
    ri = lax.broadcasted_iota(jnp.int32, (hc, hc), 0)
    ci = lax.broadcasted_iota(jnp.int32, (hc, hc), 1)
    causal = ri >= ci
    cb_tt, cb_bt, cb_bb = cb[:hc, :hc], cb[hc:, :hc], cb[hc:, hc:]
    src_rows = rows[r:, :] - jnp.log2(rows[:r, :])
    wt_cols = jnp.log2(cols[:, :r]) - cols[:, r:]
    lo = lax.broadcasted_iota(jnp.int32, (1, LANES), 1) < SSM_HEAD_DIM
    zero16 = jnp.zeros((), BF16)

    ys = []
    for p in range(r // 2):
        m_top, m_bot, cs, bw, dec = [], [], [], [], []
        for jj in range(2):
            j = 2 * p + jj
            a_b = jnp.broadcast_to(cols[:, r + j:r + j + 1], (chunk, LANES))
            w_b = jnp.broadcast_to(wt_cols[:, j:j + 1], (chunk, LANES))
            s_row = src_rows[j:j + 1, :]
            d_tt = jnp.exp2(jnp.where(causal, a_b[:hc] - s_row[:, :hc], MASK_VALUE))
            d_bt = jnp.exp2(a_b[hc:] - s_row[:, :hc])
            d_bb = jnp.exp2(jnp.where(causal, a_b[hc:] - s_row[:, hc:], MASK_VALUE))
            m_top.append((cb_tt * d_tt).astype(BF16))
            m_bot.append(jnp.concatenate([(cb_bt * d_bt).astype(BF16), (cb_bb * d_bb).astype(BF16)], axis=1))
            cs.append((cm32 * jnp.exp2(a_b)).astype(BF16))
            a_last = a_b[chunk - 1:chunk, :]
            bw.append((bm32 * jnp.exp2(a_last + w_b)).astype(BF16))
            dec.append(a_last)
        xp = x[:, p * LANES:(p + 1) * LANES]
        xlo, xhi = jnp.where(lo, xp, zero16), jnp.where(lo, zero16, xp)
        xbd = jnp.concatenate([xlo, xhi], axis=0)
        st = st_ref[:, p * LANES:(p + 1) * LANES]
        stb = st.astype(BF16)
        stbd = jnp.concatenate([jnp.where(lo, stb, zero16), jnp.where(lo, zero16, stb)], axis=0)
        y_top = jnp.dot(jnp.concatenate(m_top, axis=1), jnp.concatenate([xlo[:hc], xhi[:hc]], axis=0),
                        preferred_element_type=F32)
        y_bot = jnp.dot(jnp.concatenate(m_bot, axis=1), xbd, preferred_element_type=F32)
        y = (jnp.concatenate([y_top, y_bot], axis=0)
             + jnp.dot(jnp.concatenate(cs, axis=1), stbd, preferred_element_type=F32))
        upd = lax.dot_general(jnp.concatenate(bw, axis=0), xbd, (((0,), (0,)), ((), ())),
                              preferred_element_type=F32)
        st_ref[:, p * LANES:(p + 1) * LANES] = st * jnp.exp2(jnp.where(lo, dec[0], dec[1])) + upd
        ys.append(y)

    y = jnp.concatenate(ys, axis=1) + x.astype(F32) * d_ref[...]
    z = z_ref[...].astype(F32)
    gated = y * _silu_from_half(0.5 * z)
    o_ref[...] = _rms_rows(gated, nw_ref[...]).astype(o_ref.dtype)


def _ssd(z, xbc, cols, rows, d_exp, norm_w, d_inner, chunk):
    s = z.shape[0]
    g = SSM_GROUPS
    gw = d_inner // g
    r = gw // SSM_HEAD_DIM
    n = SSM_STATE
    b_off = d_inner // n
    c_off = b_off + g
    kernel = functools.partial(_ssd_kernel, chunk=chunk, heads_per_group=r)
    return pl.pallas_call(
        kernel,
        out_shape=jax.ShapeDtypeStruct((s, d_inner), BF16),
        grid=(g, s // chunk),
        in_specs=[pl.BlockSpec((chunk, gw), lambda gi, c: (c, gi)),
                  pl.BlockSpec((chunk, gw), lambda gi, c: (c, gi)),
                  pl.BlockSpec((chunk, n), lambda gi, c: (c, b_off + gi)),
                  pl.BlockSpec((chunk, n), lambda gi, c: (c, c_off + gi)),
                  pl.BlockSpec((1, chunk, 2 * r), lambda gi, c: (gi, c, 0)),
                  pl.BlockSpec((1, 2 * r, chunk), lambda gi, c: (gi, 0, c)),
                  pl.BlockSpec((1, gw), lambda gi, c: (0, gi)),
                  pl.BlockSpec((1, gw), lambda gi, c: (0, gi))],
        out_specs=pl.BlockSpec((chunk, gw), lambda gi, c: (c, gi)),
        scratch_shapes=[pltpu.VMEM((n, gw), F32)],
        compiler_params=_params("arbitrary", "arbitrary"),
        name="ssd_scan",
    )(z, xbc, xbc, xbc, cols, rows, d_exp, norm_w)


def _proj_res_kernel(y_ref, w_ref, h_ref, nw_ref, ho_ref, uo_ref):
    hn = h_ref[...] + jnp.dot(y_ref[...], w_ref[...], preferred_element_type=F32)
    ho_ref[...] = hn
    uo_ref[...] = _rms_rows(hn, nw_ref[...]).astype(uo_ref.dtype)


def _proj_res(y, w, layer, h, norm_w, tm):
    s, kd = y.shape
    d = w.shape[2]
    row_spec = pl.BlockSpec((tm, d), lambda m: (m, 0))
    return pl.pallas_call(
        _proj_res_kernel,
        out_shape=(jax.ShapeDtypeStruct((s, d), F32), jax.ShapeDtypeStruct((s, d), BF16)),
        grid=(s // tm,),
        in_specs=[pl.BlockSpec((tm, kd), lambda m: (m, 0)),
                  pl.BlockSpec((None, kd, d), lambda m: (layer, 0, 0), pipeline_mode=pl.Buffered(1)),
                  row_spec,
                  pl.BlockSpec((1, d), lambda m: (0, 0))],
        out_specs=(row_spec, row_spec),
        compiler_params=_params("arbitrary"),
        name="proj_residual",
    )(y, w, h, norm_w.reshape(1, d))


def _mlp_kernel(u_ref, w1_ref, w2_ref, h_ref, *rest, emit_norm):
    if emit_norm:
        nw_ref, ho_ref, uo_ref = rest
    else:
        (ho_ref,) = rest
    j = pl.program_id(1)

    @pl.when(j == 0)
    def _():
        ho_ref[...] = h_ref[...]

    a = jnp.dot(u_ref[...], w1_ref[...], preferred_element_type=F32)
    a = jnp.square(jnp.maximum(a, 0.0)).astype(BF16)
    ho_ref[...] += jnp.dot(a, w2_ref[...], preferred_element_type=F32)

    if emit_norm:
        @pl.when(j == pl.num_programs(1) - 1)
        def _():
            uo_ref[...] = _rms_rows(ho_ref[...], nw_ref[...]).astype(uo_ref.dtype)


def _mlp(u, w1, w2, layer, h, norm_w, tm, th):
    s, d = u.shape
    hd = w1.shape[2]
    emit_norm = norm_w is not None
    row_spec = pl.BlockSpec((tm, d), lambda m, j: (m, 0))
    once_spec = pl.BlockSpec((tm, d), lambda m, j: (m, 0), pipeline_mode=pl.Buffered(1))
    in_specs = [row_spec,
                pl.BlockSpec((None, d, th), lambda m, j: (layer, 0, j)),
                pl.BlockSpec((None, th, d), lambda m, j: (layer, j, 0)),
                row_spec]
    args = [u, w1, w2, h]
    out_shape = [jax.ShapeDtypeStruct((s, d), F32)]
    out_specs = [row_spec]
    if emit_norm:
        in_specs.append(pl.BlockSpec((1, d), lambda m, j: (0, 0)))
        args.append(norm_w.reshape(1, d))
        out_shape.append(jax.ShapeDtypeStruct((s, d), BF16))
        out_specs.append(once_spec)
    out = pl.pallas_call(
        functools.partial(_mlp_kernel, emit_norm=emit_norm),
        out_shape=tuple(out_shape),
        grid=(s // tm, hd // th),
        in_specs=in_specs,
        out_specs=tuple(out_specs),
        compiler_params=_params("arbitrary", "arbitrary"),
        name="mlp_residual",
    )(*args)
    return out if emit_norm else (out[0], None)


def _att_in_kernel(u_ref, w_ref, nw_ref, o_ref, *wb_ref, n_qk_tiles):
    n = pl.program_id(0)
    wb_ref = _bf16_weight_tile(w_ref, *wb_ref)
    acc = jnp.dot(u_ref[...], wb_ref[...], preferred_element_type=F32)

    @pl.when(n >= n_qk_tiles)
    def _():
        o_ref[...] = acc.astype(o_ref.dtype)

    @pl.when(n < n_qk_tiles)
    def _():
        gi = lax.broadcasted_iota(jnp.int32, (NORM_SLAB, NORM_SLAB), 0) // ATT_QK_DIM
        gj = lax.broadcasted_iota(jnp.int32, (NORM_SLAB, NORM_SLAB), 1) // ATT_QK_DIM
        ones_bd = (gi == gj).astype(BF16)
        nw = nw_ref[...]
        for sl in range(acc.shape[1] // NORM_SLAB):
            a = acc[:, sl * NORM_SLAB:(sl + 1) * NORM_SLAB]
            ss = jnp.dot((a * a).astype(BF16), ones_bd, preferred_element_type=F32)
            out = a * lax.rsqrt(ss * (1.0 / ATT_QK_DIM) + EPS) * nw[:, sl * NORM_SLAB:(sl + 1) * NORM_SLAB]
            o_ref[:, sl * NORM_SLAB:(sl + 1) * NORM_SLAB] = out.astype(o_ref.dtype)


def _att_in_proj(u, w, layer, qk_w, n_qk, tm, tn):
    s, d = u.shape
    n_out = w.shape[2]
    n_qk_tiles = n_qk // tn
    kernel = functools.partial(_att_in_kernel, n_qk_tiles=n_qk_tiles)
    return pl.pallas_call(
        kernel,
        out_shape=jax.ShapeDtypeStruct((s, n_out), BF16),
        grid=(n_out // tn, s // tm),
        in_specs=[pl.BlockSpec((tm, d), lambda n, m: (m, 0)),
                  _weight_tile_spec(layer, d, tn, 0),
                  pl.BlockSpec((1, tn), lambda n, m: (0, jnp.minimum(n, n_qk_tiles - 1)))],
        out_specs=pl.BlockSpec((tm, tn), lambda n, m: (m, n)),
        scratch_shapes=_weight_scratch(w, d, tn),
        compiler_params=_params("arbitrary", "arbitrary"),
        name="att_in_proj",
    )(u, w, qk_w)


def _bias_kernel(rel_ref, far_ref, bk_ref, o_ref):
    h = pl.program_id(0)
    bk = bk_ref[...]
    far = far_ref[h]
    acc = jnp.zeros(bk.shape, F32)
    for b in range(NUM_BUCKETS):
        acc = jnp.where(bk == b, (rel_ref[b, h] - far) * LOG2E, acc)
    n_far = o_ref.shape[2] - bk.shape[1]
    o_ref[0, :, :n_far] = jnp.zeros((bk.shape[0], n_far), F32)
    o_ref[0, :, n_far:] = jnp.where(bk < 0, MASK_VALUE, acc)


def _bias_tiles(rel_bias, far_bias, buckets):
    nb, nh = rel_bias.shape
    t, band = buckets.shape
    return pl.pallas_call(
        _bias_kernel,
        out_shape=jax.ShapeDtypeStruct((nh, t, 2 * t), F32),
        grid=(nh,),
        in_specs=[pl.BlockSpec(memory_space=pltpu.SMEM),
                  pl.BlockSpec(memory_space=pltpu.SMEM),
                  pl.BlockSpec((t, band), lambda h: (0, 0))],
        out_specs=pl.BlockSpec((1, t, 2 * t), lambda h: (h, 0, 0)),
        compiler_params=_params("arbitrary"),
        name="att_bias_tiles",
    )(rel_bias, far_bias, buckets)


def _t5_bucket(rel):
    nb = NUM_BUCKETS // 2
    ret = (rel > 0).astype(jnp.int32) * nb
    n = jnp.abs(rel)
    max_exact = nb // 2
    nf = jnp.maximum(n, max_exact).astype(F32)
    large = max_exact + (jnp.log(nf / max_exact) / math.log(MAX_DISTANCE / max_exact)
                         * (nb - max_exact)).astype(jnp.int32)
    large = jnp.minimum(large, nb - 1)
    return ret + jnp.where(n < max_exact, n, large)


def _stack_maps(q):
    lo = lax.broadcasted_iota(jnp.int32, (1, LANES), 1) < ATT_QK_DIM
    zero16 = jnp.zeros((), BF16)
    return jnp.concatenate([jnp.where(lo, q, zero16), jnp.where(lo, zero16, q)], axis=0)


def _attn_finish(acc, l, lam, sw, out_scale, t, o_ref):
    o = acc / l
    o = o[:t] - lam * o[t:]
    o_ref[...] = (_rms_rows(o, sw) * out_scale).astype(o_ref.dtype)


def _attn_fixed_kernel(lam_ref, shift_ref, q_ref, k_ref, v_ref, bias_ref, sw_ref, o_ref, l_ref, acc_ref,
                       *, tile, far_widths, out_scale):
    shift = shift_ref[pl.program_id(0)]
    qi = pl.program_id(1)
    t = tile
    qq = _stack_maps(q_ref[...])
    l_ref[...] = jnp.zeros_like(l_ref)
    acc_ref[...] = jnp.zeros_like(acc_ref)

    def step(first_tile, n_tiles, bias):
        width = n_tiles * t
        start = pl.multiple_of(first_tile * t, t)
        k = k_ref[pl.ds(start, width), :]
        v = v_ref[pl.ds(start, width), :]
        s = lax.dot_general(qq, k, (((1,), (1,)), ((), ())), preferred_element_type=F32)
        if bias is None:
            p = jnp.exp2(s - shift)
        else:
            b = bias - shift
            p = jnp.exp2(s + jnp.concatenate([b, b], axis=0))
        part = p[:, :LANES]
        for c in range(1, width // LANES):
            part = part + p[:, c * LANES:(c + 1) * LANES]
        l_ref[...] += part
        acc_ref[...] += jnp.dot(p.astype(BF16), v, preferred_element_type=F32)

    n_far = jnp.maximum(qi - 1, 0)
    w0 = far_widths[0]

    def wide_body(i, carry):
        step(i * w0, w0, None)
        return carry

    lax.fori_loop(0, n_far // w0, wide_body, 0)
    for w in far_widths[1:]:
        @pl.when((n_far & w) != 0)
        def _(w=w):
            step(n_far & ~(2 * w - 1), w, None)

    @pl.when(qi > 0)
    def _():
        step(qi - 1, 2, bias_ref[0])

    @pl.when(qi == 0)
    def _():
        step(0, 1, bias_ref[0, :, t:])

    l = jnp.sum(l_ref[...], axis=-1, keepdims=True)
    _attn_finish(acc_ref[...], l, lam_ref[0], sw_ref[...], out_scale, t, o_ref)


def _attn_online_kernel(lam_ref, q_ref, k_ref, v_ref, bias_ref, sw_ref, o_ref, m_ref, l_ref, acc_ref,
                        *, tile, out_scale):
    qi = pl.program_id(1)
    t = tile
    qq = _stack_maps(q_ref[...])
    m_ref[...] = jnp.full_like(m_ref, MASK_VALUE)
    l_ref[...] = jnp.zeros_like(l_ref)
    acc_ref[...] = jnp.zeros_like(acc_ref)

    def step(kt, bias):
        start = pl.multiple_of(kt * t, t)
        k = k_ref[pl.ds(start, t), :]
        v = v_ref[pl.ds(start, t), :]
        s = lax.dot_general(qq, k, (((1,), (1,)), ((), ())), preferred_element_type=F32)
        if bias is not None:
            s = s + jnp.concatenate([bias, bias], axis=0)
        m_old = m_ref[...]
        m_new = jnp.maximum(m_old, jnp.max(s, axis=-1, keepdims=True))
        alpha = jnp.exp2(m_old - m_new)
        p = jnp.exp2(s - m_new)
        l_ref[...] = alpha * l_ref[...] + jnp.sum(p, axis=-1, keepdims=True)
        acc_ref[...] = alpha * acc_ref[...] + jnp.dot(p.astype(BF16), v, preferred_element_type=F32)
        m_ref[...] = m_new

    def far_body(kt, carry):
        step(kt, None)
        return carry

    lax.fori_loop(0, jnp.maximum(qi - 1, 0), far_body, 0)

    @pl.when(qi > 0)
    def _():
        step(qi - 1, bias_ref[0, :, :t])

    step(qi, bias_ref[0, :, t:])
    _attn_finish(acc_ref[...], l_ref[...], lam_ref[0], sw_ref[...], out_scale, t, o_ref)


def _attention(lam, shift, qkv, bias, subln_w, n_heads, tile, out_scale):
    s = qkv.shape[0]
    smem = pl.BlockSpec(memory_space=pltpu.SMEM)
    in_specs = [pl.BlockSpec((tile, LANES), lambda h, i: (i, h)),
                pl.BlockSpec((s, LANES), lambda h, i: (0, n_heads + h)),
                pl.BlockSpec((s, LANES), lambda h, i: (0, 2 * n_heads + h)),
                pl.BlockSpec((1, tile, 2 * tile), lambda h, i: (h, 0, 0)),
                pl.BlockSpec((1, ATT_V_DIM), lambda h, i: (0, 0))]
    args = (qkv, qkv, qkv, bias, subln_w.reshape(1, ATT_V_DIM))
    acc = pltpu.VMEM((2 * tile, ATT_V_DIM), F32)
    if shift is not None:
        kernel = functools.partial(_attn_fixed_kernel, tile=tile, far_widths=ATT_FAR_WIDTHS, out_scale=out_scale)
        in_specs = [smem, smem] + in_specs
        args = (lam, shift) + args
        scratch = [pltpu.VMEM((2 * tile, LANES), F32), acc]
        name = "diff_attention"
    else:
        kernel = functools.partial(_attn_online_kernel, tile=tile, out_scale=out_scale)
        in_specs = [smem] + in_specs
        args = (lam,) + args
        scratch = [pltpu.VMEM((2 * tile, 1), F32), pltpu.VMEM((2 * tile, 1), F32), acc]
        name = "diff_attention_online"
    return pl.pallas_call(
        kernel,
        out_shape=jax.ShapeDtypeStruct((s, n_heads * ATT_V_DIM), BF16),
        grid=(n_heads, s // tile),
        in_specs=in_specs,
        out_specs=pl.BlockSpec((tile, ATT_V_DIM), lambda h, i: (i, h)),
        scratch_shapes=scratch,
        compiler_params=_params("arbitrary", "arbitrary"),
        name=name,
    )(*args)


def kernel(x, norm_mix_w, norm_mlp_w, ssm_in_w, ssm_conv_w, ssm_conv_b, ssm_dt_bias, ssm_a_log, ssm_d, ssm_norm_w, ssm_out_w, att_in_w, att_q_norm_w, att_k_norm_w, att_lam_q1, att_lam_k1, att_lam_q2, att_lam_k2, att_subln_w, att_out_w, rel_bias, mlp_w1, mlp_w2):
    bsz, s, d = x.shape
    assert bsz == 1
    depth = norm_mix_w.shape[0]
    d_inner = ssm_out_w.shape[1]
    n_ssm_heads = ssm_dt_bias.shape[1]
    conv_dim = ssm_conv_w.shape[2]
    n_att_heads = rel_bias.shape[1]
    r = n_ssm_heads // SSM_GROUPS

    tm = min(ROW_TILE, s)
    tmr = min(RES_ROW_TILE, s)
    chunk = min(SSD_CHUNK, s)
    tile = min(ATT_TILE, s)
    assert s % tm == 0 and s % tmr == 0 and s % chunk == 0 and s % tile == 0
    assert tile % ATT_CHUNK == 0 and tile >= LANES

    h = x.reshape(s, d)
    u = _rmsnorm(h, norm_mix_w[0], tmr)

    ti = jnp.arange(tile, dtype=jnp.int32)[:, None]
    tj = jnp.arange(tile + LANES, dtype=jnp.int32)[None, :] - LANES
    allowed = (tj < 0) | ((tj // ATT_CHUNK) <= (ti // ATT_CHUNK))
    buckets = jnp.where(allowed, _t5_bucket(tj - ti), -1)
    far_bias = rel_bias[_t5_bucket(jnp.int32(-2 * tile))]
    bias = _bias_tiles(rel_bias, far_bias, buckets)
    near_bias = rel_bias.astype(F32) - far_bias[None, :]
    near_hi = jnp.maximum(jnp.max(near_bias, axis=0), 0.0)
    near_lo = jnp.minimum(jnp.min(near_bias, axis=0), 0.0)

    ssm_in_wb = ssm_in_w.astype(BF16)
    ssm_out_wb, att_out_wb = ssm_out_w.astype(BF16), att_out_w.astype(BF16)
    mlp_w1b, mlp_w2b = mlp_w1.astype(BF16), mlp_w2.astype(BF16)

    for i in range(depth):
        j = i // 2
        if i % 2 == 0:
            w_dt = ssm_in_w[j, :, d_inner + conv_dim:]
            z = _proj(u, ssm_in_wb, j, 0, d_inner, tm, COL_TILE)
            xbc = _ssm_xbc_proj(u, ssm_in_wb, j, 0.5 * ssm_conv_w[j], 0.5 * ssm_conv_b[j].reshape(1, conv_dim),
                                d_inner, tm, COL_TILE)
            a = -jnp.exp(ssm_a_log[j].astype(F32))
            mult = jnp.concatenate([jnp.ones_like(a), a * LOG2E]).reshape(1, -1)
            dt_b = jnp.tile(ssm_dt_bias[j].astype(F32), 2).reshape(1, -1)
            nar = _dt_proj(u, jnp.concatenate([w_dt, w_dt], axis=1), dt_b, mult, chunk, tm)
            cols = nar.reshape(s, 2, SSM_GROUPS, r).transpose(2, 0, 1, 3).reshape(SSM_GROUPS, s, 2 * r)
            rows = cols.transpose(0, 2, 1)
            d_exp = jnp.repeat(ssm_d[j].astype(F32), SSM_HEAD_DIM).reshape(1, d_inner)
            y = _ssd(z, xbc, cols, rows, d_exp, ssm_norm_w[j].reshape(1, d_inner), d_inner, chunk)
            h, u = _proj_res(y, ssm_out_wb, j, h, norm_mlp_w[i], tmr)
        else:
            lambda_init = 0.8 - 0.6 * math.exp(-0.3 * i)
            lam = (jnp.exp(jnp.sum(att_lam_q1[j].astype(F32) * att_lam_k1[j].astype(F32)))
                   - jnp.exp(jnp.sum(att_lam_q2[j].astype(F32) * att_lam_k2[j].astype(F32)))
                   + lambda_init).reshape(1)
            reps = d // ATT_QK_DIM
            wq = att_q_norm_w[j].astype(F32)
            wk = att_k_norm_w[j].astype(F32)
            qk_w = jnp.concatenate([jnp.tile(wq, reps) * (ATT_QK_DIM ** -0.5 * LOG2E),
                                    jnp.tile(wk, reps)]).reshape(1, 2 * d)
            qkv = _att_in_proj(u, att_in_w, j, qk_w, 2 * d, tm, COL_TILE)
            logit_bound = (ATT_QK_DIM ** 0.5) * jnp.max(jnp.abs(wq * wk))
            shift = (logit_bound + near_hi) * LOG2E
            spread = jnp.max((2.0 * logit_bound + near_hi - near_lo) * LOG2E)
            att_args = (qkv, bias, att_subln_w[j], n_att_heads, tile, 1.0 - lambda_init)
            o = lax.cond(spread <= EXP2_SAFE_RANGE,
                         lambda: _attention(lam, shift, *att_args),
                         lambda: _attention(lam, None, *att_args))
            h, u = _proj_res(o, att_out_wb, j, h, norm_mlp_w[i], tmr)
        nw_next = norm_mix_w[i + 1] if i + 1 < depth else None
        h, u = _mlp(u, mlp_w1b, mlp_w2b, i, h, nw_next, tm, MLP_HIDDEN_TILE)
    return h.reshape(bsz, s, d)
```

```python
import functools
import math

import jax
import jax.numpy as jnp
from jax import lax
from jax.experimental import pallas as pl
from jax.experimental.pallas import tpu as pltpu

F32 = jnp.float32
BF16 = jnp.bfloat16

EPS = 1e-6
MASK_VALUE = -1e30

SSM_HEAD_DIM = 64
SSM_GROUPS = 8
SSM_STATE = 128
SSM_CONV = 4
ATT_QK_DIM = 64
ATT_V_DIM = 128
ATT_CHUNK = 64
NUM_BUCKETS = 32
MAX_DISTANCE = 128

LANES = 128
SUBLANES = 8
MXU_COLS = 512
VMEM_LIMIT_BYTES = 56 * 1024 * 1024

ROW_TILE = 1024
COL_TILE = 1024
RES_ROW_TILE = 512
MLP_HIDDEN_TILE = 512
SSD_CHUNK = 256
ATT_TILE = 512
ATT_FAR_WIDTHS = (4, 2, 1)

LOG2E = 1.4426950408889634
EXP2_SAFE_RANGE = 120.0
NORM_SLAB = 256


def _params(*sem):
    return pltpu.CompilerParams(dimension_semantics=sem, vmem_limit_bytes=VMEM_LIMIT_BYTES)


def _silu_from_half(h):
    return h + h * jnp.tanh(h)


def _rms_rows(x, w):
    ms = jnp.mean(x * x, axis=-1, keepdims=True)
    return x * lax.rsqrt(ms + EPS) * w


def _rmsnorm_kernel(x_ref, w_ref, o_ref):
    o_ref[...] = _rms_rows(x_ref[...], w_ref[...]).astype(o_ref.dtype)


def _rmsnorm(x, w, tm):
    s, d = x.shape
    return pl.pallas_call(
        _rmsnorm_kernel,
        out_shape=jax.ShapeDtypeStruct((s, d), BF16),
        grid=(s // tm,),
        in_specs=[pl.BlockSpec((tm, d), lambda i: (i, 0)),
                  pl.BlockSpec((1, d), lambda i: (0, 0))],
        out_specs=pl.BlockSpec((tm, d), lambda i: (i, 0)),
        compiler_params=_params("arbitrary"),
        name="rmsnorm_in",
    )(x, w.reshape(1, d))


def _bf16_weight_tile(w_ref, wb_ref=None):
    if w_ref.dtype == BF16:
        return w_ref

    @pl.when(pl.program_id(1) == 0)
    def _():
        wb_ref[...] = w_ref[...].astype(BF16)
    return wb_ref


def _weight_scratch(w, d, tn):
    return [] if w.dtype == BF16 else [pltpu.VMEM((d, tn), BF16)]


def _weight_tile_spec(layer, d, tn, c0):
    return pl.BlockSpec((None, d, tn), lambda n, m: (layer, 0, c0 + n))


def _proj_kernel(u_ref, w_ref, o_ref, *wb_ref, scale):
    wb_ref = _bf16_weight_tile(w_ref, *wb_ref)
    acc = jnp.dot(u_ref[...], wb_ref[...], preferred_element_type=F32)
    o_ref[...] = (acc * scale).astype(o_ref.dtype)


def _proj(u, w, layer, col0, n_out, scale, tm, tn):
    s, d = u.shape
    return pl.pallas_call(
        functools.partial(_proj_kernel, scale=scale),
        out_shape=jax.ShapeDtypeStruct((s, n_out), BF16),
        grid=(n_out // tn, s // tm),
        in_specs=[pl.BlockSpec((tm, d), lambda n, m: (m, 0)),
                  _weight_tile_spec(layer, d, tn, col0 // tn)],
        out_specs=pl.BlockSpec((tm, tn), lambda n, m: (m, n)),
        scratch_shapes=_weight_scratch(w, d, tn),
        compiler_params=_params("arbitrary", "arbitrary"),
        name="ssm_z_proj",
    )(u, w)


def _ssm_xbc_kernel(u_ref, w_ref, cw_ref, cb_ref, o_ref, carry_ref, *wb_ref):
    tm = u_ref.shape[0]
    groups = tm // SUBLANES
    wb_ref = _bf16_weight_tile(w_ref, *wb_ref)

    @pl.when(pl.program_id(1) == 0)
    def _():
        carry_ref[...] = jnp.zeros_like(carry_ref)

    u = u_ref[...]
    sub = lax.broadcasted_iota(jnp.int32, (1, SUBLANES, MXU_COLS), 1)
    for c in range(o_ref.shape[1] // MXU_COLS):
        cols = slice(c * MXU_COLS, (c + 1) * MXU_COLS)
        acc = jnp.dot(u, wb_ref[:, cols], preferred_element_type=F32)
        cw = cw_ref[:, cols]
        a3 = jnp.concatenate([carry_ref[:, cols], acc], axis=0).reshape(groups + 1, SUBLANES, MXU_COLS)
        half = a3[1:] * cw[SSM_CONV - 1:SSM_CONV, :] + cb_ref[:, cols]
        rot = a3
        for j in range(1, SSM_CONV):
            rot = pltpu.roll(rot, 1, axis=1)
            shifted = jnp.where(sub >= j, rot[1:], rot[:-1])
            half = half + shifted * cw[SSM_CONV - 1 - j:SSM_CONV - j, :]
        o_ref[:, cols] = _silu_from_half(half).reshape(tm, MXU_COLS).astype(o_ref.dtype)
        carry_ref[:, cols] = acc[tm - SUBLANES:, :]


def _ssm_xbc_proj(u, w, layer, conv_w, conv_b, col0, tm, tn):
    s, d = u.shape
    n_out = conv_w.shape[1]
    return pl.pallas_call(
        _ssm_xbc_kernel,
        out_shape=jax.ShapeDtypeStruct((s, n_out), BF16),
        grid=(n_out // tn, s // tm),
        in_specs=[pl.BlockSpec((tm, d), lambda n, m: (m, 0)),
                  _weight_tile_spec(layer, d, tn, col0 // tn),
                  pl.BlockSpec((SSM_CONV, tn), lambda n, m: (0, n)),
                  pl.BlockSpec((1, tn), lambda n, m: (0, n))],
        out_specs=pl.BlockSpec((tm, tn), lambda n, m: (m, n)),
        scratch_shapes=[pltpu.VMEM((SUBLANES, tn), F32)] + _weight_scratch(w, d, tn),
        compiler_params=_params("arbitrary", "arbitrary"),
        name="ssm_xbc_proj",
    )(u, w, conv_w, conv_b)


def _dt_kernel(u_ref, w_ref, b_ref, mult_ref, o_ref, *, chunk, n_heads):
    x = jnp.dot(u_ref[...], w_ref[...].astype(BF16), preferred_element_type=F32) + b_ref[...]
    sp = jnp.maximum(x, 0.0) + jnp.log1p(jnp.exp(-jnp.abs(x)))
    v = sp * mult_ref[...]
    ri = lax.broadcasted_iota(jnp.int32, (chunk, chunk), 0)
    ci = lax.broadcasted_iota(jnp.int32, (chunk, chunk), 1)
    tri = (ri >= ci).astype(F32)
    is_da = lax.broadcasted_iota(jnp.int32, (1, v.shape[1]), 1) >= n_heads
    for c in range(v.shape[0] // chunk):
        blk = v[c * chunk:(c + 1) * chunk]
        cs = jnp.dot(tri, blk, precision=lax.Precision.HIGHEST, preferred_element_type=F32)
        o_ref[c * chunk:(c + 1) * chunk, :] = jnp.where(is_da, cs, blk)


def _dt_proj(u, w2, b2, mult, chunk, tm):
    s, d = u.shape
    n = w2.shape[1]
    kernel = functools.partial(_dt_kernel, chunk=chunk, n_heads=n // 2)
    return pl.pallas_call(
        kernel,
        out_shape=jax.ShapeDtypeStruct((s, n), F32),
        grid=(s // tm,),
        in_specs=[pl.BlockSpec((tm, d), lambda i: (i, 0)),
                  pl.BlockSpec((d, n), lambda i: (0, 0)),
                  pl.BlockSpec((1, n), lambda i: (0, 0)),
                  pl.BlockSpec((1, n), lambda i: (0, 0))],
        out_specs=pl.BlockSpec((tm, n), lambda i: (i, 0)),
        compiler_params=_params("arbitrary"),
        name="ssm_dt_proj",
    )(u, w2, b2, mult)


def _ssd_kernel(z_ref, x_ref, b_ref, c_ref, cols_ref, rows_ref, d_ref, nw_ref, o_ref, st_ref,
                *, chunk, heads_per_group):
    @pl.when(pl.program_id(1) == 0)
    def _():
        st_ref[...] = jnp.zeros_like(st_ref)

    r = heads_per_group
    x = x_ref[...]
    bm = b_ref[...]
    cm = c_ref[...]
    bm32 = bm.astype(F32)
    cm32 = cm.astype(F32)
    cols = cols_ref[0]
    rows = rows_ref[0]
    cb = lax.dot_general(cm, bm, (((1,), (1,)), ((), ())), preferred_element_type=F32)
    hc = chunk // 2
    ri = lax.broadcasted_iota(jnp.int32, (hc, hc), 0)
    ci = lax.broadcasted_iota(jnp.int32, (hc, hc), 1)
    causal = ri >= ci
    cb_tt, cb_bt, cb_bb = cb[:hc, :hc], cb[hc:, :hc], cb[hc:, hc:]
    src_rows = rows[r:, :] - jnp.log2(rows[:r, :])
    bm_t = bm32.T
    a_end = rows[r:, chunk - 1:chunk]
    wt_rows = jnp.exp2(a_end - src_rows)
    lo = lax.broadcasted_iota(jnp.int32, (1, LANES), 1) < SSM_HEAD_DIM
    zero16 = jnp.zeros((), BF16)

    ys = []
    for p in range(r // 2):
        m_top, m_bot, cs, bw, dec = [], [], [], [], []
        for jj in range(2):
            j = 2 * p + jj
            a_b = jnp.broadcast_to(cols[:, r + j:r + j + 1], (chunk, LANES))
            s_row = src_rows[j:j + 1, :]
            d_tt = jnp.exp2(jnp.where(causal, a_b[:hc] - s_row[:, :hc], MASK_VALUE))
            d_bt = jnp.exp2(a_b[hc:] - s_row[:, :hc])
            d_bb = jnp.exp2(jnp.where(causal, a_b[hc:] - s_row[:, hc:], MASK_VALUE))
            m_top.append((cb_tt * d_tt).astype(BF16))
            m_bot.append(jnp.concatenate([(cb_bt * d_bt).astype(BF16), (cb_bb * d_bb).astype(BF16)], axis=1))
            cs.append((cm32 * jnp.exp2(a_b)).astype(BF16))
            bw.append((bm_t * wt_rows[j:j + 1, :]).astype(BF16))
            dec.append(jnp.exp2(a_end[j:j + 1, :]))
        xp = x[:, p * LANES:(p + 1) * LANES]
        xlo, xhi = jnp.where(lo, xp, zero16), jnp.where(lo, zero16, xp)
        xbd = jnp.concatenate([xlo, xhi], axis=0)
        st = st_ref[:, p * LANES:(p + 1) * LANES]
        stb = st.astype(BF16)
        stbd = jnp.concatenate([jnp.where(lo, stb, zero16), jnp.where(lo, zero16, stb)], axis=0)
        y_top = jnp.dot(jnp.concatenate(m_top, axis=1), jnp.concatenate([xlo[:hc], xhi[:hc]], axis=0),
                        preferred_element_type=F32)
        y_bot = jnp.dot(jnp.concatenate(m_bot, axis=1), xbd, preferred_element_type=F32)
        y = (jnp.concatenate([y_top, y_bot], axis=0)
             + jnp.dot(jnp.concatenate(cs, axis=1), stbd, preferred_element_type=F32))
        upd = jnp.dot(jnp.concatenate(bw, axis=1), xbd, preferred_element_type=F32)
        st_ref[:, p * LANES:(p + 1) * LANES] = st * jnp.where(lo, dec[0], dec[1]) + upd
        ys.append(y)

    y = jnp.concatenate(ys, axis=1) + x.astype(F32) * d_ref[...]
    z = z_ref[...].astype(F32)
    gated = y * _silu_from_half(z)
    o_ref[...] = _rms_rows(gated, nw_ref[...]).astype(o_ref.dtype)


def _ssd(z, xbc, cols, rows, d_exp, norm_w, d_inner, chunk):
    s = z.shape[0]
    g = SSM_GROUPS
    gw = d_inner // g
    r = gw // SSM_HEAD_DIM
    n = SSM_STATE
    b_off = d_inner // n
    c_off = b_off + g
    kernel = functools.partial(_ssd_kernel, chunk=chunk, heads_per_group=r)
    return pl.pallas_call(
        kernel,
        out_shape=jax.ShapeDtypeStruct((s, d_inner), BF16),
        grid=(g, s // chunk),
        in_specs=[pl.BlockSpec((chunk, gw), lambda gi, c: (c, gi)),
                  pl.BlockSpec((chunk, gw), lambda gi, c: (c, gi)),
                  pl.BlockSpec((chunk, n), lambda gi, c: (c, b_off + gi)),
                  pl.BlockSpec((chunk, n), lambda gi, c: (c, c_off + gi)),
                  pl.BlockSpec((1, chunk, 2 * r), lambda gi, c: (gi, c, 0)),
                  pl.BlockSpec((1, 2 * r, chunk), lambda gi, c: (gi, 0, c)),
                  pl.BlockSpec((1, gw), lambda gi, c: (0, gi)),
                  pl.BlockSpec((1, gw), lambda gi, c: (0, gi))],
        out_specs=pl.BlockSpec((chunk, gw), lambda gi, c: (c, gi)),
        scratch_shapes=[pltpu.VMEM((n, gw), F32)],
        compiler_params=_params("arbitrary", "arbitrary"),
        name="ssd_scan",
    )(z, xbc, xbc, xbc, cols, rows, d_exp, norm_w)


def _proj_res_kernel(y_ref, w_ref, h_ref, nw_ref, ho_ref, uo_ref):
    hn = h_ref[...] + jnp.dot(y_ref[...], w_ref[...], preferred_element_type=F32)
    ho_ref[...] = hn
    uo_ref[...] = _rms_rows(hn, nw_ref[...]).astype(uo_ref.dtype)


def _proj_res(y, w, layer, h, norm_w, tm):
    s, kd = y.shape
    d = w.shape[2]
    row_spec = pl.BlockSpec((tm, d), lambda m: (m, 0))
    return pl.pallas_call(
        _proj_res_kernel,
        out_shape=(jax.ShapeDtypeStruct((s, d), F32), jax.ShapeDtypeStruct((s, d), BF16)),
        grid=(s // tm,),
        in_specs=[pl.BlockSpec((tm, kd), lambda m: (m, 0)),
                  pl.BlockSpec((None, kd, d), lambda m: (layer, 0, 0), pipeline_mode=pl.Buffered(1)),
                  row_spec,
                  pl.BlockSpec((1, d), lambda m: (0, 0))],
        out_specs=(row_spec, row_spec),
        compiler_params=_params("arbitrary"),
        name="proj_residual",
    )(y, w, h, norm_w.reshape(1, d))


def _mlp_kernel(u_ref, w1_ref, w2_ref, h_ref, *rest, emit_norm):
    if emit_norm:
        nw_ref, ho_ref, uo_ref = rest
    else:
        (ho_ref,) = rest
    j = pl.program_id(1)

    @pl.when(j == 0)
    def _():
        ho_ref[...] = h_ref[...]

    a = jnp.dot(u_ref[...], w1_ref[...], preferred_element_type=F32)
    a = jnp.square(jnp.maximum(a, 0.0)).astype(BF16)
    ho_ref[...] += jnp.dot(a, w2_ref[...], preferred_element_type=F32)

    if emit_norm:
        @pl.when(j == pl.num_programs(1) - 1)
        def _():
            uo_ref[...] = _rms_rows(ho_ref[...], nw_ref[...]).astype(uo_ref.dtype)


def _mlp(u, w1, w2, layer, h, norm_w, tm, th):
    s, d = u.shape
    hd = w1.shape[2]
    emit_norm = norm_w is not None
    row_spec = pl.BlockSpec((tm, d), lambda m, j: (m, 0))
    once_spec = pl.BlockSpec((tm, d), lambda m, j: (m, 0), pipeline_mode=pl.Buffered(1))
    in_specs = [row_spec,
                pl.BlockSpec((None, d, th), lambda m, j: (layer, 0, j)),
                pl.BlockSpec((None, th, d), lambda m, j: (layer, j, 0)),
                row_spec]
    args = [u, w1, w2, h]
    out_shape = [jax.ShapeDtypeStruct((s, d), F32)]
    out_specs = [row_spec]
    if emit_norm:
        in_specs.append(pl.BlockSpec((1, d), lambda m, j: (0, 0)))
        args.append(norm_w.reshape(1, d))
        out_shape.append(jax.ShapeDtypeStruct((s, d), BF16))
        out_specs.append(once_spec)
    out = pl.pallas_call(
        functools.partial(_mlp_kernel, emit_norm=emit_norm),
        out_shape=tuple(out_shape),
        grid=(s // tm, hd // th),
        in_specs=in_specs,
        out_specs=tuple(out_specs),
        compiler_params=_params("arbitrary", "arbitrary"),
        name="mlp_residual",
    )(*args)
    return out if emit_norm else (out[0], None)


def _att_in_kernel(u_ref, w_ref, nw_ref, o_ref, *wb_ref, n_qk_tiles):
    n = pl.program_id(0)
    wb_ref = _bf16_weight_tile(w_ref, *wb_ref)
    acc = jnp.dot(u_ref[...], wb_ref[...], preferred_element_type=F32)

    @pl.when(n >= n_qk_tiles)
    def _():
        o_ref[...] = acc.astype(o_ref.dtype)

    @pl.when(n < n_qk_tiles)
    def _():
        gi = lax.broadcasted_iota(jnp.int32, (NORM_SLAB, NORM_SLAB), 0) // ATT_QK_DIM
        gj = lax.broadcasted_iota(jnp.int32, (NORM_SLAB, NORM_SLAB), 1) // ATT_QK_DIM
        ones_bd = (gi == gj).astype(BF16)
        nw = nw_ref[...]
        for sl in range(acc.shape[1] // NORM_SLAB):
            a = acc[:, sl * NORM_SLAB:(sl + 1) * NORM_SLAB]
            ss = jnp.dot((a * a).astype(BF16), ones_bd, preferred_element_type=F32)
            out = a * lax.rsqrt(ss * (1.0 / ATT_QK_DIM) + EPS) * nw[:, sl * NORM_SLAB:(sl + 1) * NORM_SLAB]
            o_ref[:, sl * NORM_SLAB:(sl + 1) * NORM_SLAB] = out.astype(o_ref.dtype)


def _att_in_proj(u, w, layer, qk_w, n_qk, tm, tn):
    s, d = u.shape
    n_out = w.shape[2]
    n_qk_tiles = n_qk // tn
    kernel = functools.partial(_att_in_kernel, n_qk_tiles=n_qk_tiles)
    return pl.pallas_call(
        kernel,
        out_shape=jax.ShapeDtypeStruct((s, n_out), BF16),
        grid=(n_out // tn, s // tm),
        in_specs=[pl.BlockSpec((tm, d), lambda n, m: (m, 0)),
                  _weight_tile_spec(layer, d, tn, 0),
                  pl.BlockSpec((1, tn), lambda n, m: (0, jnp.minimum(n, n_qk_tiles - 1)))],
        out_specs=pl.BlockSpec((tm, tn), lambda n, m: (m, n)),
        scratch_shapes=_weight_scratch(w, d, tn),
        compiler_params=_params("arbitrary", "arbitrary"),
        name="att_in_proj",
    )(u, w, qk_w)


def _bias_kernel(rel_ref, far_ref, bk_ref, o_ref):
    h = pl.program_id(0)
    bk = bk_ref[...]
    far = far_ref[h]
    acc = jnp.zeros(bk.shape, F32)
    for b in range(NUM_BUCKETS):
        acc = jnp.where(bk == b, (rel_ref[b, h] - far) * LOG2E, acc)
    n_far = o_ref.shape[2] - bk.shape[1]
    o_ref[0, :, :n_far] = jnp.zeros((bk.shape[0], n_far), F32)
    o_ref[0, :, n_far:] = jnp.where(bk < 0, MASK_VALUE, acc)


def _bias_tiles(rel_bias, far_bias, buckets):
    nb, nh = rel_bias.shape
    t, band = buckets.shape
    return pl.pallas_call(
        _bias_kernel,
        out_shape=jax.ShapeDtypeStruct((nh, t, 2 * t), F32),
        grid=(nh,),
        in_specs=[pl.BlockSpec(memory_space=pltpu.SMEM),
                  pl.BlockSpec(memory_space=pltpu.SMEM),
                  pl.BlockSpec((t, band), lambda h: (0, 0))],
        out_specs=pl.BlockSpec((1, t, 2 * t), lambda h: (h, 0, 0)),
        compiler_params=_params("arbitrary"),
        name="att_bias_tiles",
    )(rel_bias, far_bias, buckets)


def _t5_bucket(rel):
    nb = NUM_BUCKETS // 2
    ret = (rel > 0).astype(jnp.int32) * nb
    n = jnp.abs(rel)
    max_exact = nb // 2
    nf = jnp.maximum(n, max_exact).astype(F32)
    large = max_exact + (jnp.log(nf / max_exact) / math.log(MAX_DISTANCE / max_exact)
                         * (nb - max_exact)).astype(jnp.int32)
    large = jnp.minimum(large, nb - 1)
    return ret + jnp.where(n < max_exact, n, large)


def _stack_maps(q):
    lo = lax.broadcasted_iota(jnp.int32, (1, LANES), 1) < ATT_QK_DIM
    zero16 = jnp.zeros((), BF16)
    return jnp.concatenate([jnp.where(lo, q, zero16), jnp.where(lo, zero16, q)], axis=0)


def _attn_finish(acc, l, lam, sw, out_scale, t, o_ref):
    o = acc / l
    o = o[:t] - lam * o[t:]
    o_ref[...] = (_rms_rows(o, sw) * out_scale).astype(o_ref.dtype)


def _attn_fixed_kernel(lam_ref, shift_ref, q_ref, k_ref, v_ref, bias_ref, sw_ref, o_ref, l_ref, acc_ref,
                       *, tile, far_widths, out_scale):
    shift = shift_ref[pl.program_id(0)]
    qi = pl.program_id(1)
    t = tile
    qq = _stack_maps(q_ref[...])
    l_ref[...] = jnp.zeros_like(l_ref)
    acc_ref[...] = jnp.zeros_like(acc_ref)

    def step(first_tile, n_tiles, bias):
        width = n_tiles * t
        start = pl.multiple_of(first_tile * t, t)
        k = k_ref[pl.ds(start, width), :]
        v = v_ref[pl.ds(start, width), :]
        s = lax.dot_general(qq, k, (((1,), (1,)), ((), ())), preferred_element_type=F32)
        if bias is None:
            p = jnp.exp2(s - shift)
        else:
            b = bias - shift
            p = jnp.exp2(s + jnp.concatenate([b, b], axis=0))
        part = p[:, :LANES]
        for c in range(1, width // LANES):
            part = part + p[:, c * LANES:(c + 1) * LANES]
        l_ref[...] += part
        acc_ref[...] += jnp.dot(p.astype(BF16), v, preferred_element_type=F32)

    n_far = jnp.maximum(qi - 1, 0)
    w0 = far_widths[0]

    def wide_body(i, carry):
        step(i * w0, w0, None)
        return carry

    lax.fori_loop(0, n_far // w0, wide_body, 0)
    for w in far_widths[1:]:
        @pl.when((n_far & w) != 0)
        def _(w=w):
            step(n_far & ~(2 * w - 1), w, None)

    @pl.when(qi > 0)
    def _():
        step(qi - 1, 2, bias_ref[0])

    @pl.when(qi == 0)
    def _():
        step(0, 1, bias_ref[0, :, t:])

    l = jnp.sum(l_ref[...], axis=-1, keepdims=True)
    _attn_finish(acc_ref[...], l, lam_ref[0], sw_ref[...], out_scale, t, o_ref)


def _attn_online_kernel(lam_ref, q_ref, k_ref, v_ref, bias_ref, sw_ref, o_ref, m_ref, l_ref, acc_ref,
                        *, tile, out_scale):
    qi = pl.program_id(1)
    t = tile
    qq = _stack_maps(q_ref[...])
    m_ref[...] = jnp.full_like(m_ref, MASK_VALUE)
    l_ref[...] = jnp.zeros_like(l_ref)
    acc_ref[...] = jnp.zeros_like(acc_ref)

    def step(kt, bias):
        start = pl.multiple_of(kt * t, t)
        k = k_ref[pl.ds(start, t), :]
        v = v_ref[pl.ds(start, t), :]
        s = lax.dot_general(qq, k, (((1,), (1,)), ((), ())), preferred_element_type=F32)
        if bias is not None:
            s = s + jnp.concatenate([bias, bias], axis=0)
        m_old = m_ref[...]
        m_new = jnp.maximum(m_old, jnp.max(s, axis=-1, keepdims=True))
        alpha = jnp.exp2(m_old - m_new)
        p = jnp.exp2(s - m_new)
        l_ref[...] = alpha * l_ref[...] + jnp.sum(p, axis=-1, keepdims=True)
        acc_ref[...] = alpha * acc_ref[...] + jnp.dot(p.astype(BF16), v, preferred_element_type=F32)
        m_ref[...] = m_new

    def far_body(kt, carry):
        step(kt, None)
        return carry

    lax.fori_loop(0, jnp.maximum(qi - 1, 0), far_body, 0)

    @pl.when(qi > 0)
    def _():
        step(qi - 1, bias_ref[0, :, :t])

    step(qi, bias_ref[0, :, t:])
    _attn_finish(acc_ref[...], l_ref[...], lam_ref[0], sw_ref[...], out_scale, t, o_ref)


def _attention(lam, shift, qkv, bias, subln_w, n_heads, tile, out_scale):
    s = qkv.shape[0]
    smem = pl.BlockSpec(memory_space=pltpu.SMEM)
    in_specs = [pl.BlockSpec((tile, LANES), lambda h, i: (i, h)),
                pl.BlockSpec((s, LANES), lambda h, i: (0, n_heads + h)),
                pl.BlockSpec((s, LANES), lambda h, i: (0, 2 * n_heads + h)),
                pl.BlockSpec((1, tile, 2 * tile), lambda h, i: (h, 0, 0)),
                pl.BlockSpec((1, ATT_V_DIM), lambda h, i: (0, 0))]
    args = (qkv, qkv, qkv, bias, subln_w.reshape(1, ATT_V_DIM))
    acc = pltpu.VMEM((2 * tile, ATT_V_DIM), F32)
    if shift is not None:
        kernel = functools.partial(_attn_fixed_kernel, tile=tile, far_widths=ATT_FAR_WIDTHS, out_scale=out_scale)
        in_specs = [smem, smem] + in_specs
        args = (lam, shift) + args
        scratch = [pltpu.VMEM((2 * tile, LANES), F32), acc]
        name = "diff_attention"
    else:
        kernel = functools.partial(_attn_online_kernel, tile=tile, out_scale=out_scale)
        in_specs = [smem] + in_specs
        args = (lam,) + args
        scratch = [pltpu.VMEM((2 * tile, 1), F32), pltpu.VMEM((2 * tile, 1), F32), acc]
        name = "diff_attention_online"
    return pl.pallas_call(
        kernel,
        out_shape=jax.ShapeDtypeStruct((s, n_heads * ATT_V_DIM), BF16),
        grid=(n_heads, s // tile),
        in_specs=in_specs,
        out_specs=pl.BlockSpec((tile, ATT_V_DIM), lambda h, i: (i, h)),
        scratch_shapes=scratch,
        compiler_params=_params("arbitrary", "arbitrary"),
        name=name,
    )(*args)


def kernel(x, norm_mix_w, norm_mlp_w, ssm_in_w, ssm_conv_w, ssm_conv_b, ssm_dt_bias, ssm_a_log, ssm_d, ssm_norm_w, ssm_out_w, att_in_w, att_q_norm_w, att_k_norm_w, att_lam_q1, att_lam_k1, att_lam_q2, att_lam_k2, att_subln_w, att_out_w, rel_bias, mlp_w1, mlp_w2):
    bsz, s, d = x.shape
    assert bsz == 1
    depth = norm_mix_w.shape[0]
    d_inner = ssm_out_w.shape[1]
    n_ssm_heads = ssm_dt_bias.shape[1]
    conv_dim = ssm_conv_w.shape[2]
    n_att_heads = rel_bias.shape[1]
    r = n_ssm_heads // SSM_GROUPS

    tm = min(ROW_TILE, s)
    tmr = min(RES_ROW_TILE, s)
    chunk = min(SSD_CHUNK, s)
    tile = min(ATT_TILE, s)
    assert s % tm == 0 and s % tmr == 0 and s % chunk == 0 and s % tile == 0
    assert tile % ATT_CHUNK == 0 and tile >= LANES

    h = x.reshape(s, d)
    u = _rmsnorm(h, norm_mix_w[0], tmr)

    ti = jnp.arange(tile, dtype=jnp.int32)[:, None]
    tj = jnp.arange(tile + LANES, dtype=jnp.int32)[None, :] - LANES
    allowed = (tj < 0) | ((tj // ATT_CHUNK) <= (ti // ATT_CHUNK))
    buckets = jnp.where(allowed, _t5_bucket(tj - ti), -1)
    far_bias = rel_bias[_t5_bucket(jnp.int32(-2 * tile))]
    bias = _bias_tiles(rel_bias, far_bias, buckets)
    near_bias = rel_bias.astype(F32) - far_bias[None, :]
    near_hi = jnp.maximum(jnp.max(near_bias, axis=0), 0.0)
    near_lo = jnp.minimum(jnp.min(near_bias, axis=0), 0.0)

    ssm_in_wb = ssm_in_w.astype(BF16)
    ssm_out_wb, att_out_wb = ssm_out_w.astype(BF16), att_out_w.astype(BF16)
    mlp_w1b, mlp_w2b = mlp_w1.astype(BF16), mlp_w2.astype(BF16)

    for i in range(depth):
        j = i // 2
        if i % 2 == 0:
            w_dt = ssm_in_wb[j, :, d_inner + conv_dim:]
            z = _proj(u, ssm_in_wb, j, 0, d_inner, 0.5, tm, COL_TILE)
            xbc = _ssm_xbc_proj(u, ssm_in_wb, j, 0.5 * ssm_conv_w[j], 0.5 * ssm_conv_b[j].reshape(1, conv_dim),
                                d_inner, tm, COL_TILE)
            a = -jnp.exp(ssm_a_log[j].astype(F32))
            mult = jnp.concatenate([jnp.ones_like(a), a * LOG2E]).reshape(1, -1)
            dt_b = jnp.tile(ssm_dt_bias[j].astype(F32), 2).reshape(1, -1)
            nar = _dt_proj(u, jnp.concatenate([w_dt, w_dt], axis=1), dt_b, mult, chunk, tm)
            cols = nar.reshape(s, 2, SSM_GROUPS, r).transpose(2, 0, 1, 3).reshape(SSM_GROUPS, s, 2 * r)
            rows = cols.transpose(0, 2, 1)
            d_exp = jnp.repeat(ssm_d[j].astype(F32), SSM_HEAD_DIM).reshape(1, d_inner)
            y = _ssd(z, xbc, cols, rows, d_exp, ssm_norm_w[j].reshape(1, d_inner), d_inner, chunk)
            h, u = _proj_res(y, ssm_out_wb, j, h, norm_mlp_w[i], tmr)
        else:
            lambda_init = 0.8 - 0.6 * math.exp(-0.3 * i)
            lam = (jnp.exp(jnp.sum(att_lam_q1[j].astype(F32) * att_lam_k1[j].astype(F32)))
                   - jnp.exp(jnp.sum(att_lam_q2[j].astype(F32) * att_lam_k2[j].astype(F32)))
                   + lambda_init).reshape(1)
            reps = d // ATT_QK_DIM
            wq = att_q_norm_w[j].astype(F32)
            wk = att_k_norm_w[j].astype(F32)
            qk_w = jnp.concatenate([jnp.tile(wq, reps) * (ATT_QK_DIM ** -0.5 * LOG2E),
                                    jnp.tile(wk, reps)]).reshape(1, 2 * d)
            qkv = _att_in_proj(u, att_in_w, j, qk_w, 2 * d, tm, COL_TILE)
            logit_bound = (ATT_QK_DIM ** 0.5) * jnp.max(jnp.abs(wq * wk))
            shift = (logit_bound + near_hi) * LOG2E
            spread = jnp.max((2.0 * logit_bound + near_hi - near_lo) * LOG2E)
            att_args = (qkv, bias, att_subln_w[j], n_att_heads, tile, 1.0 - lambda_init)
            o = lax.cond(spread <= EXP2_SAFE_RANGE,
                         lambda: _attention(lam, shift, *att_args),
                         lambda: _attention(lam, None, *att_args))
            h, u = _proj_res(o, att_out_wb, j, h, norm_mlp_w[i], tmr)
        nw_next = norm_mix_w[i + 1] if i + 1 < depth else None
        h, u = _mlp(u, mlp_w1b, mlp_w2b, i, h, nw_next, tm, MLP_HIDDEN_TILE)
    return h.reshape(bsz, s, d)
```

```python
import functools
import math

import jax
import jax.numpy as jnp
from jax import lax
from jax.experimental import pallas as pl
from jax.experimental.pallas import tpu as pltpu

F32 = jnp.float32
BF16 = jnp.bfloat16

EPS = 1e-6
MASK_VALUE = -1e30

SSM_HEAD_DIM = 64
SSM_GROUPS = 8
SSM_STATE = 128
SSM_CONV = 4
ATT_QK_DIM = 64
ATT_V_DIM = 128
ATT_CHUNK = 64
NUM_BUCKETS = 32
MAX_DISTANCE = 128

LANES = 128
SUBLANES = 8
MXU_COLS = 512
VMEM_LIMIT_BYTES = 56 * 1024 * 1024

ROW_TILE = 1024
COL_TILE = 1024
XBC_COL_TILE = 2048
RES_ROW_TILE = 512
MLP_HIDDEN_TILE = 512
SSD_CHUNK = 256
ATT_TILE = 512
ATT_FAR_WIDTHS = (8, 4, 2, 1)
BIAS_BAND = LANES

LOG2E = 1.4426950408889634
EXP2_SAFE_RANGE = 120.0
NORM_SLAB = 256


def _params(*sem):
    return pltpu.CompilerParams(dimension_semantics=sem, vmem_limit_bytes=VMEM_LIMIT_BYTES)


def _silu_from_half(h):
    return h + h * jnp.tanh(h)


def _rms_rows(x, w):
    ms = jnp.mean(x * x, axis=-1, keepdims=True)
    return x * lax.rsqrt(ms + EPS) * w


def _rmsnorm_kernel(x_ref, w_ref, o_ref):
    o_ref[...] = _rms_rows(x_ref[...], w_ref[...]).astype(o_ref.dtype)


def _rmsnorm(x, w, tm):
    s, d = x.shape
    return pl.pallas_call(
        _rmsnorm_kernel,
        out_shape=jax.ShapeDtypeStruct((s, d), BF16),
        grid=(s // tm,),
        in_specs=[pl.BlockSpec((tm, d), lambda i: (i, 0)),
                  pl.BlockSpec((1, d), lambda i: (0, 0))],
        out_specs=pl.BlockSpec((tm, d), lambda i: (i, 0)),
        compiler_params=_params("arbitrary"),
        name="rmsnorm_in",
    )(x, w.reshape(1, d))


def _bf16_weight_tile(w_ref, wb_ref=None):
    if w_ref.dtype == BF16:
        return w_ref

    @pl.when(pl.program_id(1) == 0)
    def _():
        wb_ref[...] = w_ref[...].astype(BF16)
    return wb_ref


def _weight_scratch(w, d, tn):
    return [] if w.dtype == BF16 else [pltpu.VMEM((d, tn), BF16)]


def _weight_tile_spec(layer, d, tn, c0):
    return pl.BlockSpec((None, d, tn), lambda n, m: (layer, 0, c0 + n))


def _proj_kernel(u_ref, w_ref, o_ref, *wb_ref, scale):
    wb_ref = _bf16_weight_tile(w_ref, *wb_ref)
    acc = jnp.dot(u_ref[...], wb_ref[...], preferred_element_type=F32)
    o_ref[...] = (acc * scale).astype(o_ref.dtype)


def _proj(u, w, layer, col0, n_out, scale, tm, tn):
    s, d = u.shape
    return pl.pallas_call(
        functools.partial(_proj_kernel, scale=scale),
        out_shape=jax.ShapeDtypeStruct((s, n_out), BF16),
        grid=(n_out // tn, s // tm),
        in_specs=[pl.BlockSpec((tm, d), lambda n, m: (m, 0)),
                  _weight_tile_spec(layer, d, tn, col0 // tn)],
        out_specs=pl.BlockSpec((tm, tn), lambda n, m: (m, n)),
        scratch_shapes=_weight_scratch(w, d, tn),
        compiler_params=_params("arbitrary", "arbitrary"),
        name="ssm_z_proj",
    )(u, w)


def _ssm_xbc_kernel(u_ref, w_ref, cw_ref, cb_ref, o_ref, carry_ref, *wb_ref):
    tm = u_ref.shape[0]
    groups = tm // SUBLANES
    wb_ref = _bf16_weight_tile(w_ref, *wb_ref)

    @pl.when(pl.program_id(1) == 0)
    def _():
        carry_ref[...] = jnp.zeros_like(carry_ref)

    u = u_ref[...]
    sub = lax.broadcasted_iota(jnp.int32, (1, SUBLANES, MXU_COLS), 1)
    for c in range(o_ref.shape[1] // MXU_COLS):
        cols = slice(c * MXU_COLS, (c + 1) * MXU_COLS)
        acc = jnp.dot(u, wb_ref[:, cols], preferred_element_type=F32)
        cw = cw_ref[:, cols]
        a3 = jnp.concatenate([carry_ref[:, cols], acc], axis=0).reshape(groups + 1, SUBLANES, MXU_COLS)
        half = a3[1:] * cw[SSM_CONV - 1:SSM_CONV, :] + cb_ref[:, cols]
        rot = a3
        for j in range(1, SSM_CONV):
            rot = pltpu.roll(rot, 1, axis=1)
            shifted = jnp.where(sub >= j, rot[1:], rot[:-1])
            half = half + shifted * cw[SSM_CONV - 1 - j:SSM_CONV - j, :]
        o_ref[:, cols] = _silu_from_half(half).reshape(tm, MXU_COLS).astype(o_ref.dtype)
        carry_ref[:, cols] = acc[tm - SUBLANES:, :]


def _ssm_xbc_proj(u, w, layer, conv_w, conv_b, col0, tm, tn):
    s, d = u.shape
    n_out = conv_w.shape[1]
    return pl.pallas_call(
        _ssm_xbc_kernel,
        out_shape=jax.ShapeDtypeStruct((s, n_out), BF16),
        grid=(n_out // tn, s // tm),
        in_specs=[pl.BlockSpec((tm, d), lambda n, m: (m, 0)),
                  _weight_tile_spec(layer, d, tn, col0 // tn),
                  pl.BlockSpec((SSM_CONV, tn), lambda n, m: (0, n)),
                  pl.BlockSpec((1, tn), lambda n, m: (0, n))],
        out_specs=pl.BlockSpec((tm, tn), lambda n, m: (m, n)),
        scratch_shapes=[pltpu.VMEM((SUBLANES, tn), F32)] + _weight_scratch(w, d, tn),
        compiler_params=_params("arbitrary", "arbitrary"),
        name="ssm_xbc_proj",
    )(u, w, conv_w, conv_b)


def _dt_kernel(u_ref, w_ref, b_ref, mult_ref, o_ref, *, chunk, n_heads):
    x = jnp.dot(u_ref[...], w_ref[...].astype(BF16), preferred_element_type=F32) + b_ref[...]
    sp = jnp.maximum(x, 0.0) + jnp.log1p(jnp.exp(-jnp.abs(x)))
    v = sp * mult_ref[...]
    ri = lax.broadcasted_iota(jnp.int32, (chunk, chunk), 0)
    ci = lax.broadcasted_iota(jnp.int32, (chunk, chunk), 1)
    tri = (ri >= ci).astype(F32)
    is_da = lax.broadcasted_iota(jnp.int32, (1, v.shape[1]), 1) >= n_heads
    for c in range(v.shape[0] // chunk):
        blk = v[c * chunk:(c + 1) * chunk]
        cs = jnp.dot(tri, blk, precision=lax.Precision.HIGHEST, preferred_element_type=F32)
        o_ref[c * chunk:(c + 1) * chunk, :] = jnp.where(is_da, cs, blk)


def _dt_proj(u, w2, b2, mult, chunk, tm):
    s, d = u.shape
    n = w2.shape[1]
    kernel = functools.partial(_dt_kernel, chunk=chunk, n_heads=n // 2)
    return pl.pallas_call(
        kernel,
        out_shape=jax.ShapeDtypeStruct((s, n), F32),
        grid=(s // tm,),
        in_specs=[pl.BlockSpec((tm, d), lambda i: (i, 0)),
                  pl.BlockSpec((d, n), lambda i: (0, 0)),
                  pl.BlockSpec((1, n), lambda i: (0, 0)),
                  pl.BlockSpec((1, n), lambda i: (0, 0))],
        out_specs=pl.BlockSpec((tm, n), lambda i: (i, 0)),
        compiler_params=_params("arbitrary"),
        name="ssm_dt_proj",
    )(u, w2, b2, mult)


def _ssd_kernel(z_ref, x_ref, b_ref, c_ref, cols_ref, rows_ref, d_ref, nw_ref, o_ref, st_ref,
                *, chunk, heads_per_group):
    @pl.when(pl.program_id(1) == 0)
    def _():
        st_ref[...] = jnp.zeros_like(st_ref)

    r = heads_per_group
    x = x_ref[...]
    bm = b_ref[...]
    cm = c_ref[...]
    bm32 = bm.astype(F32)
    cm32 = cm.astype(F32)
    cols = cols_ref[0]
    rows = rows_ref[0]
    cb = lax.dot_general(cm, bm, (((1,), (1,)), ((), ())), preferred_element_type=F32)
    hc = chunk // 2
    ri = lax.broadcasted_iota(jnp.int32, (hc, hc), 0)
    ci = lax.broadcasted_iota(jnp.int32, (hc, hc), 1)
    causal = ri >= ci
    cb_tt, cb_bt, cb_bb = cb[:hc, :hc], cb[hc:, :hc], cb[hc:, hc:]
    src_rows = rows[r:, :] - jnp.log2(rows[:r, :])
    bm_t = bm32.T
    a_end = rows[r:, chunk - 1:chunk]
    wt_rows = jnp.exp2(a_end - src_rows)
    lo = lax.broadcasted_iota(jnp.int32, (1, LANES), 1) < SSM_HEAD_DIM
    zero16 = jnp.zeros((), BF16)

    ys = []
    for p in range(r // 2):
        m_top, m_bot, cs, bw, dec = [], [], [], [], []
        for jj in range(2):
            j = 2 * p + jj
            a_b = jnp.broadcast_to(cols[:, r + j:r + j + 1], (chunk, LANES))
            s_row = src_rows[j:j + 1, :]
            d_tt = jnp.exp2(jnp.where(causal, a_b[:hc] - s_row[:, :hc], MASK_VALUE))
            d_bt = jnp.exp2(a_b[hc:] - s_row[:, :hc])
            d_bb = jnp.exp2(jnp.where(causal, a_b[hc:] - s_row[:, hc:], MASK_VALUE))
            m_top.append((cb_tt * d_tt).astype(BF16))
            m_bot.append(jnp.concatenate([(cb_bt * d_bt).astype(BF16), (cb_bb * d_bb).astype(BF16)], axis=1))
            cs.append((cm32 * jnp.exp2(a_b)).astype(BF16))
            bw.append((bm_t * wt_rows[j:j + 1, :]).astype(BF16))
            dec.append(jnp.exp2(a_end[j:j + 1, :]))
        xp = x[:, p * LANES:(p + 1) * LANES]
        xlo, xhi = jnp.where(lo, xp, zero16), jnp.where(lo, zero16, xp)
        xbd = jnp.concatenate([xlo, xhi], axis=0)
        st = st_ref[:, p * LANES:(p + 1) * LANES]
        stb = st.astype(BF16)
        stbd = jnp.concatenate([jnp.where(lo, stb, zero16), jnp.where(lo, zero16, stb)], axis=0)
        y_top = jnp.dot(jnp.concatenate(m_top, axis=1), jnp.concatenate([xlo[:hc], xhi[:hc]], axis=0),
                        preferred_element_type=F32)
        y_bot = jnp.dot(jnp.concatenate(m_bot, axis=1), xbd, preferred_element_type=F32)
        y = (jnp.concatenate([y_top, y_bot], axis=0)
             + jnp.dot(jnp.concatenate(cs, axis=1), stbd, preferred_element_type=F32))
        upd = jnp.dot(jnp.concatenate(bw, axis=1), xbd, preferred_element_type=F32)
        st_ref[:, p * LANES:(p + 1) * LANES] = st * jnp.where(lo, dec[0], dec[1]) + upd
        ys.append(y)

    y = jnp.concatenate(ys, axis=1) + x.astype(F32) * d_ref[...]
    z = z_ref[...].astype(F32)
    gated = y * _silu_from_half(z)
    o_ref[...] = _rms_rows(gated, nw_ref[...]).astype(o_ref.dtype)


def _ssd(z, xbc, cols, rows, d_exp, norm_w, d_inner, chunk):
    s = z.shape[0]
    g = SSM_GROUPS
    gw = d_inner // g
    r = gw // SSM_HEAD_DIM
    n = SSM_STATE
    b_off = d_inner // n
    c_off = b_off + g
    kernel = functools.partial(_ssd_kernel, chunk=chunk, heads_per_group=r)
    return pl.pallas_call(
        kernel,
        out_shape=jax.ShapeDtypeStruct((s, d_inner), BF16),
        grid=(g, s // chunk),
        in_specs=[pl.BlockSpec((chunk, gw), lambda gi, c: (c, gi)),
                  pl.BlockSpec((chunk, gw), lambda gi, c: (c, gi)),
                  pl.BlockSpec((chunk, n), lambda gi, c: (c, b_off + gi)),
                  pl.BlockSpec((chunk, n), lambda gi, c: (c, c_off + gi)),
                  pl.BlockSpec((1, chunk, 2 * r), lambda gi, c: (gi, c, 0)),
                  pl.BlockSpec((1, 2 * r, chunk), lambda gi, c: (gi, 0, c)),
                  pl.BlockSpec((1, gw), lambda gi, c: (0, gi)),
                  pl.BlockSpec((1, gw), lambda gi, c: (0, gi))],
        out_specs=pl.BlockSpec((chunk, gw), lambda gi, c: (c, gi)),
        scratch_shapes=[pltpu.VMEM((n, gw), F32)],
        compiler_params=_params("arbitrary", "arbitrary"),
        name="ssd_scan",
    )(z, xbc, xbc, xbc, cols, rows, d_exp, norm_w)


def _proj_res_kernel(y_ref, w_ref, h_ref, nw_ref, ho_ref, uo_ref):
    hn = h_ref[...] + jnp.dot(y_ref[...], w_ref[...], preferred_element_type=F32)
    ho_ref[...] = hn
    uo_ref[...] = _rms_rows(hn, nw_ref[...]).astype(uo_ref.dtype)


def _proj_res(y, w, layer, h, norm_w, tm):
    s, kd = y.shape
    d = w.shape[2]
    row_spec = pl.BlockSpec((tm, d), lambda m: (m, 0))
    return pl.pallas_call(
        _proj_res_kernel,
        out_shape=(jax.ShapeDtypeStruct((s, d), F32), jax.ShapeDtypeStruct((s, d), BF16)),
        grid=(s // tm,),
        in_specs=[pl.BlockSpec((tm, kd), lambda m: (m, 0)),
                  pl.BlockSpec((None, kd, d), lambda m: (layer, 0, 0), pipeline_mode=pl.Buffered(1)),
                  row_spec,
                  pl.BlockSpec((1, d), lambda m: (0, 0))],
        out_specs=(row_spec, row_spec),
        compiler_params=_params("arbitrary"),
        name="proj_residual",
    )(y, w, h, norm_w.reshape(1, d))


def _mlp_kernel(u_ref, w1_ref, w2_ref, h_ref, *rest, emit_norm):
    if emit_norm:
        nw_ref, ho_ref, uo_ref = rest
    else:
        (ho_ref,) = rest
    j = pl.program_id(1)

    @pl.when(j == 0)
    def _():
        ho_ref[...] = h_ref[...]

    a = jnp.dot(u_ref[...], w1_ref[...], preferred_element_type=F32)
    a = jnp.square(jnp.maximum(a, 0.0)).astype(BF16)
    ho_ref[...] += jnp.dot(a, w2_ref[...], preferred_element_type=F32)

    if emit_norm:
        @pl.when(j == pl.num_programs(1) - 1)
        def _():
            uo_ref[...] = _rms_rows(ho_ref[...], nw_ref[...]).astype(uo_ref.dtype)


def _mlp(u, w1, w2, layer, h, norm_w, tm, th):
    s, d = u.shape
    hd = w1.shape[2]
    emit_norm = norm_w is not None
    row_spec = pl.BlockSpec((tm, d), lambda m, j: (m, 0))
    once_spec = pl.BlockSpec((tm, d), lambda m, j: (m, 0), pipeline_mode=pl.Buffered(1))
    in_specs = [row_spec,
                pl.BlockSpec((None, d, th), lambda m, j: (layer, 0, j)),
                pl.BlockSpec((None, th, d), lambda m, j: (layer, j, 0)),
                row_spec]
    args = [u, w1, w2, h]
    out_shape = [jax.ShapeDtypeStruct((s, d), F32)]
    out_specs = [row_spec]
    if emit_norm:
        in_specs.append(pl.BlockSpec((1, d), lambda m, j: (0, 0)))
        args.append(norm_w.reshape(1, d))
        out_shape.append(jax.ShapeDtypeStruct((s, d), BF16))
        out_specs.append(once_spec)
    out = pl.pallas_call(
        functools.partial(_mlp_kernel, emit_norm=emit_norm),
        out_shape=tuple(out_shape),
        grid=(s // tm, hd // th),
        in_specs=in_specs,
        out_specs=tuple(out_specs),
        compiler_params=_params("arbitrary", "arbitrary"),
        name="mlp_residual",
    )(*args)
    return out if emit_norm else (out[0], None)


def _att_in_kernel(u_ref, w_ref, nw_ref, o_ref, *wb_ref, n_qk_tiles):
    n = pl.program_id(0)
    wb_ref = _bf16_weight_tile(w_ref, *wb_ref)
    acc = jnp.dot(u_ref[...], wb_ref[...], preferred_element_type=F32)

    @pl.when(n >= n_qk_tiles)
    def _():
        o_ref[...] = acc.astype(o_ref.dtype)

    @pl.when(n < n_qk_tiles)
    def _():
        gi = lax.broadcasted_iota(jnp.int32, (NORM_SLAB, NORM_SLAB), 0) // ATT_QK_DIM
        gj = lax.broadcasted_iota(jnp.int32, (NORM_SLAB, NORM_SLAB), 1) // ATT_QK_DIM
        ones_bd = (gi == gj).astype(BF16)
        nw = nw_ref[...]
        for sl in range(acc.shape[1] // NORM_SLAB):
            a = acc[:, sl * NORM_SLAB:(sl + 1) * NORM_SLAB]
            ss = jnp.dot((a * a).astype(BF16), ones_bd, preferred_element_type=F32)
            out = a * lax.rsqrt(ss * (1.0 / ATT_QK_DIM) + EPS) * nw[:, sl * NORM_SLAB:(sl + 1) * NORM_SLAB]
            o_ref[:, sl * NORM_SLAB:(sl + 1) * NORM_SLAB] = out.astype(o_ref.dtype)


def _att_in_proj(u, w, layer, qk_w, n_qk, tm, tn):
    s, d = u.shape
    n_out = w.shape[2]
    n_qk_tiles = n_qk // tn
    kernel = functools.partial(_att_in_kernel, n_qk_tiles=n_qk_tiles)
    return pl.pallas_call(
        kernel,
        out_shape=jax.ShapeDtypeStruct((s, n_out), BF16),
        grid=(n_out // tn, s // tm),
        in_specs=[pl.BlockSpec((tm, d), lambda n, m: (m, 0)),
                  _weight_tile_spec(layer, d, tn, 0),
                  pl.BlockSpec((1, tn), lambda n, m: (0, jnp.minimum(n, n_qk_tiles - 1)))],
        out_specs=pl.BlockSpec((tm, tn), lambda n, m: (m, n)),
        scratch_shapes=_weight_scratch(w, d, tn),
        compiler_params=_params("arbitrary", "arbitrary"),
        name="att_in_proj",
    )(u, w, qk_w)


def _bias_kernel(rel_ref, far_ref, bk_ref, o_ref):
    h = pl.program_id(0)
    bk = bk_ref[...]
    far = far_ref[h]
    acc = jnp.zeros(bk.shape, F32)
    for b in range(NUM_BUCKETS):
        acc = jnp.where(bk == b, (rel_ref[b, h] - far) * LOG2E, acc)
    n_far = o_ref.shape[2] - bk.shape[1]
    o_ref[0, :, :n_far] = jnp.zeros((bk.shape[0], n_far), F32)
    o_ref[0, :, n_far:] = jnp.where(bk < 0, MASK_VALUE, acc)


def _bias_tiles(rel_bias, far_bias, buckets):
    nb, nh = rel_bias.shape
    t, band = buckets.shape
    return pl.pallas_call(
        _bias_kernel,
        out_shape=jax.ShapeDtypeStruct((nh, t, 2 * t), F32),
        grid=(nh,),
        in_specs=[pl.BlockSpec(memory_space=pltpu.SMEM),
                  pl.BlockSpec(memory_space=pltpu.SMEM),
                  pl.BlockSpec((t, band), lambda h: (0, 0))],
        out_specs=pl.BlockSpec((1, t, 2 * t), lambda h: (h, 0, 0)),
        compiler_params=_params("arbitrary"),
        name="att_bias_tiles",
    )(rel_bias, far_bias, buckets)


def _t5_bucket(rel):
    nb = NUM_BUCKETS // 2
    ret = (rel > 0).astype(jnp.int32) * nb
    n = jnp.abs(rel)
    max_exact = nb // 2
    nf = jnp.maximum(n, max_exact).astype(F32)
    large = max_exact + (jnp.log(nf / max_exact) / math.log(MAX_DISTANCE / max_exact)
                         * (nb - max_exact)).astype(jnp.int32)
    large = jnp.minimum(large, nb - 1)
    return ret + jnp.where(n < max_exact, n, large)


def _stack_maps(q):
    lo = lax.broadcasted_iota(jnp.int32, (1, LANES), 1) < ATT_QK_DIM
    zero16 = jnp.zeros((), BF16)
    return jnp.concatenate([jnp.where(lo, q, zero16), jnp.where(lo, zero16, q)], axis=0)


def _attn_finish(acc, l, lam, sw, out_scale, t, o_ref):
    o = acc / l
    o = o[:t] - lam * o[t:]
    o_ref[...] = (_rms_rows(o, sw) * out_scale).astype(o_ref.dtype)


def _attn_fixed_kernel(lam_ref, shift_ref, q_ref, k_ref, v_ref, bias_ref, sw_ref, o_ref, l_ref, acc_ref,
                       *, tile, far_widths, out_scale):
    shift = shift_ref[pl.program_id(0)]
    qi = pl.program_id(1)
    t = tile
    qq = _stack_maps(q_ref[...])
    l_ref[...] = jnp.zeros_like(l_ref)
    acc_ref[...] = jnp.zeros_like(acc_ref)

    def step(first_tile, n_tiles, bias):
        width = n_tiles * t
        start = pl.multiple_of(first_tile * t, t)
        k = k_ref[pl.ds(start, width), :]
        v = v_ref[pl.ds(start, width), :]
        s = lax.dot_general(qq, k, (((1,), (1,)), ((), ())), preferred_element_type=F32)
        if bias is None:
            p = jnp.exp2(s - shift)
        else:
            b = bias - shift
            p = jnp.exp2(s + jnp.concatenate([b, b], axis=0))
        part = p[:, :LANES]
        for c in range(1, width // LANES):
            part = part + p[:, c * LANES:(c + 1) * LANES]
        l_ref[...] += part
        acc_ref[...] += jnp.dot(p.astype(BF16), v, preferred_element_type=F32)

    n_far = jnp.maximum(qi - 1, 0)
    w0 = far_widths[0]

    def wide_body(i, carry):
        step(i * w0, w0, None)
        return carry

    lax.fori_loop(0, n_far // w0, wide_body, 0)
    for w in far_widths[1:]:
        @pl.when((n_far & w) != 0)
        def _(w=w):
            step(n_far & ~(2 * w - 1), w, None)

    @pl.when(qi > 0)
    def _():
        step(qi - 1, 2, bias_ref[0])

    @pl.when(qi == 0)
    def _():
        step(0, 1, bias_ref[0, :, t:])

    l = jnp.sum(l_ref[...], axis=-1, keepdims=True)
    _attn_finish(acc_ref[...], l, lam_ref[0], sw_ref[...], out_scale, t, o_ref)


def _attn_online_kernel(lam_ref, q_ref, k_ref, v_ref, bias_ref, sw_ref, o_ref, m_ref, l_ref, acc_ref,
                        *, tile, out_scale):
    qi = pl.program_id(1)
    t = tile
    qq = _stack_maps(q_ref[...])
    m_ref[...] = jnp.full_like(m_ref, MASK_VALUE)
    l_ref[...] = jnp.zeros_like(l_ref)
    acc_ref[...] = jnp.zeros_like(acc_ref)

    def step(kt, bias):
        start = pl.multiple_of(kt * t, t)
        k = k_ref[pl.ds(start, t), :]
        v = v_ref[pl.ds(start, t), :]
        s = lax.dot_general(qq, k, (((1,), (1,)), ((), ())), preferred_element_type=F32)
        if bias is not None:
            s = s + jnp.concatenate([bias, bias], axis=0)
        m_old = m_ref[...]
        m_new = jnp.maximum(m_old, jnp.max(s, axis=-1, keepdims=True))
        alpha = jnp.exp2(m_old - m_new)
        p = jnp.exp2(s - m_new)
        l_ref[...] = alpha * l_ref[...] + jnp.sum(p, axis=-1, keepdims=True)
        acc_ref[...] = alpha * acc_ref[...] + jnp.dot(p.astype(BF16), v, preferred_element_type=F32)
        m_ref[...] = m_new

    def far_body(kt, carry):
        step(kt, None)
        return carry

    lax.fori_loop(0, jnp.maximum(qi - 1, 0), far_body, 0)

    @pl.when(qi > 0)
    def _():
        step(qi - 1, bias_ref[0, :, :t])

    step(qi, bias_ref[0, :, t:])
    _attn_finish(acc_ref[...], l_ref[...], lam_ref[0], sw_ref[...], out_scale, t, o_ref)


def _attention(lam, shift, qkv, bias, subln_w, n_heads, tile, out_scale):
    s = qkv.shape[0]
    smem = pl.BlockSpec(memory_space=pltpu.SMEM)
    in_specs = [pl.BlockSpec((tile, LANES), lambda h, i: (i, h)),
                pl.BlockSpec((s, LANES), lambda h, i: (0, n_heads + h)),
                pl.BlockSpec((s, LANES), lambda h, i: (0, 2 * n_heads + h)),
                pl.BlockSpec((1, tile, 2 * tile), lambda h, i: (h, 0, 0)),
                pl.BlockSpec((1, ATT_V_DIM), lambda h, i: (0, 0))]
    args = (qkv, qkv, qkv, bias, subln_w.reshape(1, ATT_V_DIM))
    acc = pltpu.VMEM((2 * tile, ATT_V_DIM), F32)
    if shift is not None:
        kernel = functools.partial(_attn_fixed_kernel, tile=tile, far_widths=ATT_FAR_WIDTHS, out_scale=out_scale)
        in_specs = [smem, smem] + in_specs
        args = (lam, shift) + args
        scratch = [pltpu.VMEM((2 * tile, LANES), F32), acc]
        name = "diff_attention"
    else:
        kernel = functools.partial(_attn_online_kernel, tile=tile, out_scale=out_scale)
        in_specs = [smem] + in_specs
        args = (lam,) + args
        scratch = [pltpu.VMEM((2 * tile, 1), F32), pltpu.VMEM((2 * tile, 1), F32), acc]
        name = "diff_attention_online"
    return pl.pallas_call(
        kernel,
        out_shape=jax.ShapeDtypeStruct((s, n_heads * ATT_V_DIM), BF16),
        grid=(n_heads, s // tile),
        in_specs=in_specs,
        out_specs=pl.BlockSpec((tile, ATT_V_DIM), lambda h, i: (i, h)),
        scratch_shapes=scratch,
        compiler_params=_params("arbitrary", "arbitrary"),
        name=name,
    )(*args)


def kernel(x, norm_mix_w, norm_mlp_w, ssm_in_w, ssm_conv_w, ssm_conv_b, ssm_dt_bias, ssm_a_log, ssm_d, ssm_norm_w, ssm_out_w, att_in_w, att_q_norm_w, att_k_norm_w, att_lam_q1, att_lam_k1, att_lam_q2, att_lam_k2, att_subln_w, att_out_w, rel_bias, mlp_w1, mlp_w2):
    bsz, s, d = x.shape
    assert bsz == 1
    depth = norm_mix_w.shape[0]
    d_inner = ssm_out_w.shape[1]
    n_ssm_heads = ssm_dt_bias.shape[1]
    conv_dim = ssm_conv_w.shape[2]
    n_att_heads = rel_bias.shape[1]
    r = n_ssm_heads // SSM_GROUPS

    tm = min(ROW_TILE, s)
    tmr = min(RES_ROW_TILE, s)
    chunk = min(SSD_CHUNK, s)
    tile = min(ATT_TILE, s)
    assert s % tm == 0 and s % tmr == 0 and s % chunk == 0 and s % tile == 0
    assert tile % ATT_CHUNK == 0 and tile >= BIAS_BAND

    h = x.reshape(s, d)
    u = _rmsnorm(h, norm_mix_w[0], tmr)

    ti = jnp.arange(tile, dtype=jnp.int32)[:, None]
    tj = jnp.arange(tile + BIAS_BAND, dtype=jnp.int32)[None, :] - BIAS_BAND
    allowed = (tj < 0) | ((tj // ATT_CHUNK) <= (ti // ATT_CHUNK))
    buckets = jnp.where(allowed, _t5_bucket(tj - ti), -1)
    far_bias = rel_bias[_t5_bucket(jnp.int32(-2 * tile))]
    bias = _bias_tiles(rel_bias, far_bias, buckets)
    near_bias = rel_bias.astype(F32) - far_bias[None, :]
    near_hi = jnp.maximum(jnp.max(near_bias, axis=0), 0.0)
    near_lo = jnp.minimum(jnp.min(near_bias, axis=0), 0.0)

    ssm_in_wb = ssm_in_w.astype(BF16)
    ssm_out_wb, att_out_wb = ssm_out_w.astype(BF16), att_out_w.astype(BF16)
    mlp_w1b, mlp_w2b = mlp_w1.astype(BF16), mlp_w2.astype(BF16)

    for i in range(depth):
        j = i // 2
        if i % 2 == 0:
            w_dt = ssm_in_wb[j, :, d_inner + conv_dim:]
            z = _proj(u, ssm_in_wb, j, 0, d_inner, 0.5, tm, COL_TILE)
            xbc = _ssm_xbc_proj(u, ssm_in_wb, j, 0.5 * ssm_conv_w[j], 0.5 * ssm_conv_b[j].reshape(1, conv_dim),
                                d_inner, tm, XBC_COL_TILE)
            a = -jnp.exp(ssm_a_log[j].astype(F32))
            mult = jnp.concatenate([jnp.ones_like(a), a * LOG2E]).reshape(1, -1)
            dt_b = jnp.tile(ssm_dt_bias[j].astype(F32), 2).reshape(1, -1)
            nar = _dt_proj(u, jnp.concatenate([w_dt, w_dt], axis=1), dt_b, mult, chunk, tm)
            cols = nar.reshape(s, 2, SSM_GROUPS, r).transpose(2, 0, 1, 3).reshape(SSM_GROUPS, s, 2 * r)
            rows = cols.transpose(0, 2, 1)
            d_exp = jnp.repeat(ssm_d[j].astype(F32), SSM_HEAD_DIM).reshape(1, d_inner)
            y = _ssd(z, xbc, cols, rows, d_exp, ssm_norm_w[j].reshape(1, d_inner), d_inner, chunk)
            h, u = _proj_res(y, ssm_out_wb, j, h, norm_mlp_w[i], tmr)
        else:
            lambda_init = 0.8 - 0.6 * math.exp(-0.3 * i)
            lam = (jnp.exp(jnp.sum(att_lam_q1[j].astype(F32) * att_lam_k1[j].astype(F32)))
                   - jnp.exp(jnp.sum(att_lam_q2[j].astype(F32) * att_lam_k2[j].astype(F32)))
                   + lambda_init).reshape(1)
            reps = d // ATT_QK_DIM
            wq = att_q_norm_w[j].astype(F32)
            wk = att_k_norm_w[j].astype(F32)
            qk_w = jnp.concatenate([jnp.tile(wq, reps) * (ATT_QK_DIM ** -0.5 * LOG2E),
                                    jnp.tile(wk, reps)]).reshape(1, 2 * d)
            qkv = _att_in_proj(u, att_in_w, j, qk_w, 2 * d, tm, COL_TILE)
            logit_bound = (ATT_QK_DIM ** 0.5) * jnp.max(jnp.abs(wq * wk))
            shift = (logit_bound + near_hi) * LOG2E
            spread = jnp.max((2.0 * logit_bound + near_hi - near_lo) * LOG2E)
            att_args = (qkv, bias, att_subln_w[j], n_att_heads, tile, 1.0 - lambda_init)
            o = lax.cond(spread <= EXP2_SAFE_RANGE,
                         lambda: _attention(lam, shift, *att_args),
                         lambda: _attention(lam, None, *att_args))
            h, u = _proj_res(o, att_out_wb, j, h, norm_mlp_w[i], tmr)
        nw_next = norm_mix_w[i + 1] if i + 1 < depth else None
        h, u = _mlp(u, mlp_w1b, mlp_w2b, i, h, nw_next, tm, MLP_HIDDEN_TILE)
    return h.reshape(bsz, s, d)
```

```python
import functools
import math

import jax
import jax.numpy as jnp
from jax import lax
from jax.experimental import pallas as pl
from jax.experimental.pallas import tpu as pltpu

F32 = jnp.float32
BF16 = jnp.bfloat16

EPS = 1e-6
MASK_VALUE = -1e30

SSM_HEAD_DIM = 64
SSM_GROUPS = 8
SSM_STATE = 128
SSM_CONV = 4
ATT_QK_DIM = 64
ATT_V_DIM = 128
ATT_CHUNK = 64
NUM_BUCKETS = 32
MAX_DISTANCE = 128

LANES = 128
SUBLANES = 8
MXU_COLS = 512
VMEM_LIMIT_BYTES = 56 * 1024 * 1024

ROW_TILE = 1024
COL_TILE = 1024
XBC_COL_TILE = 2048
RES_ROW_TILE = 512
MLP_HIDDEN_TILE = 1024
SSD_CHUNK = 256
ATT_TILE = 512
ATT_FAR_WIDTHS = (8, 4, 2, 1)
BIAS_BAND = LANES

LOG2E = 1.4426950408889634
EXP2_SAFE_RANGE = 120.0
NORM_SLAB = 256


def _params(*sem):
    return pltpu.CompilerParams(dimension_semantics=sem, vmem_limit_bytes=VMEM_LIMIT_BYTES)


def _silu_from_half(h):
    return h + h * jnp.tanh(h)


def _rms_rows(x, w):
    ms = jnp.mean(x * x, axis=-1, keepdims=True)
    return x * lax.rsqrt(ms + EPS) * w


def _rmsnorm_kernel(x_ref, w_ref, o_ref):
    o_ref[...] = _rms_rows(x_ref[...], w_ref[...]).astype(o_ref.dtype)


def _rmsnorm(x, w, tm):
    s, d = x.shape
    return pl.pallas_call(
        _rmsnorm_kernel,
        out_shape=jax.ShapeDtypeStruct((s, d), BF16),
        grid=(s // tm,),
        in_specs=[pl.BlockSpec((tm, d), lambda i: (i, 0)),
                  pl.BlockSpec((1, d), lambda i: (0, 0))],
        out_specs=pl.BlockSpec((tm, d), lambda i: (i, 0)),
        compiler_params=_params("arbitrary"),
        name="rmsnorm_in",
    )(x, w.reshape(1, d))


def _bf16_weight_tile(w_ref, wb_ref=None):
    if w_ref.dtype == BF16:
        return w_ref

    @pl.when(pl.program_id(1) == 0)
    def _():
        wb_ref[...] = w_ref[...].astype(BF16)
    return wb_ref


def _weight_scratch(w, d, tn):
    return [] if w.dtype == BF16 else [pltpu.VMEM((d, tn), BF16)]


def _weight_tile_spec(layer, d, tn, c0):
    return pl.BlockSpec((None, d, tn), lambda n, m: (layer, 0, c0 + n))


def _proj_kernel(u_ref, w_ref, o_ref, *wb_ref, scale):
    wb_ref = _bf16_weight_tile(w_ref, *wb_ref)
    acc = jnp.dot(u_ref[...], wb_ref[...], preferred_element_type=F32)
    o_ref[...] = (acc * scale).astype(o_ref.dtype)


def _proj(u, w, layer, col0, n_out, scale, tm, tn):
    s, d = u.shape
    return pl.pallas_call(
        functools.partial(_proj_kernel, scale=scale),
        out_shape=jax.ShapeDtypeStruct((s, n_out), BF16),
        grid=(n_out // tn, s // tm),
        in_specs=[pl.BlockSpec((tm, d), lambda n, m: (m, 0)),
                  _weight_tile_spec(layer, d, tn, col0 // tn)],
        out_specs=pl.BlockSpec((tm, tn), lambda n, m: (m, n)),
        scratch_shapes=_weight_scratch(w, d, tn),
        compiler_params=_params("arbitrary", "arbitrary"),
        name="ssm_z_proj",
    )(u, w)


def _ssm_xbc_kernel(u_ref, w_ref, cw_ref, cb_ref, o_ref, carry_ref, *wb_ref):
    tm = u_ref.shape[0]
    groups = tm // SUBLANES
    wb_ref = _bf16_weight_tile(w_ref, *wb_ref)

    @pl.when(pl.program_id(1) == 0)
    def _():
        carry_ref[...] = jnp.zeros_like(carry_ref)

    u = u_ref[...]
    sub = lax.broadcasted_iota(jnp.int32, (1, SUBLANES, MXU_COLS), 1)
    for c in range(o_ref.shape[1] // MXU_COLS):
        cols = slice(c * MXU_COLS, (c + 1) * MXU_COLS)
        acc = jnp.dot(u, wb_ref[:, cols], preferred_element_type=F32)
        cw = cw_ref[:, cols]
        a3 = jnp.concatenate([carry_ref[:, cols], acc], axis=0).reshape(groups + 1, SUBLANES, MXU_COLS)
        half = a3[1:] * cw[SSM_CONV - 1:SSM_CONV, :] + cb_ref[:, cols]
        rot = a3
        for j in range(1, SSM_CONV):
            rot = pltpu.roll(rot, 1, axis=1)
            shifted = jnp.where(sub >= j, rot[1:], rot[:-1])
            half = half + shifted * cw[SSM_CONV - 1 - j:SSM_CONV - j, :]
        o_ref[:, cols] = _silu_from_half(half).reshape(tm, MXU_COLS).astype(o_ref.dtype)
        carry_ref[:, cols] = acc[tm - SUBLANES:, :]


def _ssm_xbc_proj(u, w, layer, conv_w, conv_b, col0, tm, tn):
    s, d = u.shape
    n_out = conv_w.shape[1]
    return pl.pallas_call(
        _ssm_xbc_kernel,
        out_shape=jax.ShapeDtypeStruct((s, n_out), BF16),
        grid=(n_out // tn, s // tm),
        in_specs=[pl.BlockSpec((tm, d), lambda n, m: (m, 0)),
                  _weight_tile_spec(layer, d, tn, col0 // tn),
                  pl.BlockSpec((SSM_CONV, tn), lambda n, m: (0, n)),
                  pl.BlockSpec((1, tn), lambda n, m: (0, n))],
        out_specs=pl.BlockSpec((tm, tn), lambda n, m: (m, n)),
        scratch_shapes=[pltpu.VMEM((SUBLANES, tn), F32)] + _weight_scratch(w, d, tn),
        compiler_params=_params("arbitrary", "arbitrary"),
        name="ssm_xbc_proj",
    )(u, w, conv_w, conv_b)


def _dt_kernel(u_ref, w_ref, b_ref, mult_ref, o_ref, *, chunk, n_heads):
    x = jnp.dot(u_ref[...], w_ref[...].astype(BF16), preferred_element_type=F32) + b_ref[...]
    sp = jnp.maximum(x, 0.0) + jnp.log1p(jnp.exp(-jnp.abs(x)))
    v = sp * mult_ref[...]
    ri = lax.broadcasted_iota(jnp.int32, (chunk, chunk), 0)
    ci = lax.broadcasted_iota(jnp.int32, (chunk, chunk), 1)
    tri = (ri >= ci).astype(F32)
    is_da = lax.broadcasted_iota(jnp.int32, (1, v.shape[1]), 1) >= n_heads
    for c in range(v.shape[0] // chunk):
        blk = v[c * chunk:(c + 1) * chunk]
        cs = jnp.dot(tri, blk, precision=lax.Precision.HIGHEST, preferred_element_type=F32)
        o_ref[c * chunk:(c + 1) * chunk, :] = jnp.where(is_da, cs, blk)


def _dt_proj(u, w2, b2, mult, chunk, tm):
    s, d = u.shape
    n = w2.shape[1]
    kernel = functools.partial(_dt_kernel, chunk=chunk, n_heads=n // 2)
    return pl.pallas_call(
        kernel,
        out_shape=jax.ShapeDtypeStruct((s, n), F32),
        grid=(s // tm,),
        in_specs=[pl.BlockSpec((tm, d), lambda i: (i, 0)),
                  pl.BlockSpec((d, n), lambda i: (0, 0)),
                  pl.BlockSpec((1, n), lambda i: (0, 0)),
                  pl.BlockSpec((1, n), lambda i: (0, 0))],
        out_specs=pl.BlockSpec((tm, n), lambda i: (i, 0)),
        compiler_params=_params("arbitrary"),
        name="ssm_dt_proj",
    )(u, w2, b2, mult)


def _ssd_kernel(z_ref, x_ref, b_ref, c_ref, cols_ref, rows_ref, d_ref, nw_ref, o_ref, st_ref,
                *, chunk, heads_per_group):
    @pl.when(pl.program_id(1) == 0)
    def _():
        st_ref[...] = jnp.zeros_like(st_ref)

    r = heads_per_group
    x = x_ref[...]
    bm = b_ref[...]
    cm = c_ref[...]
    bm32 = bm.astype(F32)
    cm32 = cm.astype(F32)
    cols = cols_ref[0]
    rows = rows_ref[0]
    cb = lax.dot_general(cm, bm, (((1,), (1,)), ((), ())), preferred_element_type=F32)
    hc = chunk // 2
    ri = lax.broadcasted_iota(jnp.int32, (hc, hc), 0)
    ci = lax.broadcasted_iota(jnp.int32, (hc, hc), 1)
    causal = ri >= ci
    cb_tt, cb_bt, cb_bb = cb[:hc, :hc], cb[hc:, :hc], cb[hc:, hc:]
    src_rows = rows[r:, :] - jnp.log2(rows[:r, :])
    bm_t = bm32.T
    a_end = rows[r:, chunk - 1:chunk]
    wt_rows = jnp.exp2(a_end - src_rows)
    lo = lax.broadcasted_iota(jnp.int32, (1, LANES), 1) < SSM_HEAD_DIM
    zero16 = jnp.zeros((), BF16)

    ys = []
    for p in range(r // 2):
        m_top, m_bot, cs, bw, dec = [], [], [], [], []
        for jj in range(2):
            j = 2 * p + jj
            a_b = jnp.broadcast_to(cols[:, r + j:r + j + 1], (chunk, LANES))
            s_row = src_rows[j:j + 1, :]
            d_tt = jnp.exp2(jnp.where(causal, a_b[:hc] - s_row[:, :hc], MASK_VALUE))
            d_bt = jnp.exp2(a_b[hc:] - s_row[:, :hc])
            d_bb = jnp.exp2(jnp.where(causal, a_b[hc:] - s_row[:, hc:], MASK_VALUE))
            m_top.append((cb_tt * d_tt).astype(BF16))
            m_bot.append(jnp.concatenate([(cb_bt * d_bt).astype(BF16), (cb_bb * d_bb).astype(BF16)], axis=1))
            cs.append((cm32 * jnp.exp2(a_b)).astype(BF16))
            bw.append((bm_t * wt_rows[j:j + 1, :]).astype(BF16))
            dec.append(jnp.exp2(a_end[j:j + 1, :]))
        xp = x[:, p * LANES:(p + 1) * LANES]
        xlo, xhi = jnp.where(lo, xp, zero16), jnp.where(lo, zero16, xp)
        xbd = jnp.concatenate([xlo, xhi], axis=0)
        st = st_ref[:, p * LANES:(p + 1) * LANES]
        stb = st.astype(BF16)
        stbd = jnp.concatenate([jnp.where(lo, stb, zero16), jnp.where(lo, zero16, stb)], axis=0)
        y_top = jnp.dot(jnp.concatenate(m_top, axis=1), jnp.concatenate([xlo[:hc], xhi[:hc]], axis=0),
                        preferred_element_type=F32)
        y_bot = jnp.dot(jnp.concatenate(m_bot, axis=1), xbd, preferred_element_type=F32)
        y = (jnp.concatenate([y_top, y_bot], axis=0)
             + jnp.dot(jnp.concatenate(cs, axis=1), stbd, preferred_element_type=F32))
        upd = jnp.dot(jnp.concatenate(bw, axis=1), xbd, preferred_element_type=F32)
        st_ref[:, p * LANES:(p + 1) * LANES] = st * jnp.where(lo, dec[0], dec[1]) + upd
        ys.append(y)

    y = jnp.concatenate(ys, axis=1) + x.astype(F32) * d_ref[...]
    z = z_ref[...].astype(F32)
    gated = y * _silu_from_half(z)
    o_ref[...] = _rms_rows(gated, nw_ref[...]).astype(o_ref.dtype)


def _ssd(z, xbc, cols, rows, d_exp, norm_w, d_inner, chunk):
    s = z.shape[0]
    g = SSM_GROUPS
    gw = d_inner // g
    r = gw // SSM_HEAD_DIM
    n = SSM_STATE
    b_off = d_inner // n
    c_off = b_off + g
    kernel = functools.partial(_ssd_kernel, chunk=chunk, heads_per_group=r)
    return pl.pallas_call(
        kernel,
        out_shape=jax.ShapeDtypeStruct((s, d_inner), BF16),
        grid=(g, s // chunk),
        in_specs=[pl.BlockSpec((chunk, gw), lambda gi, c: (c, gi)),
                  pl.BlockSpec((chunk, gw), lambda gi, c: (c, gi)),
                  pl.BlockSpec((chunk, n), lambda gi, c: (c, b_off + gi)),
                  pl.BlockSpec((chunk, n), lambda gi, c: (c, c_off + gi)),
                  pl.BlockSpec((1, chunk, 2 * r), lambda gi, c: (gi, c, 0)),
                  pl.BlockSpec((1, 2 * r, chunk), lambda gi, c: (gi, 0, c)),
                  pl.BlockSpec((1, gw), lambda gi, c: (0, gi)),
                  pl.BlockSpec((1, gw), lambda gi, c: (0, gi))],
        out_specs=pl.BlockSpec((chunk, gw), lambda gi, c: (c, gi)),
        scratch_shapes=[pltpu.VMEM((n, gw), F32)],
        compiler_params=_params("arbitrary", "arbitrary"),
        name="ssd_scan",
    )(z, xbc, xbc, xbc, cols, rows, d_exp, norm_w)


def _proj_res_kernel(y_ref, w_ref, h_ref, nw_ref, ho_ref, uo_ref):
    hn = h_ref[...] + jnp.dot(y_ref[...], w_ref[...], preferred_element_type=F32)
    ho_ref[...] = hn
    uo_ref[...] = _rms_rows(hn, nw_ref[...]).astype(uo_ref.dtype)


def _proj_res(y, w, layer, h, norm_w, tm):
    s, kd = y.shape
    d = w.shape[2]
    row_spec = pl.BlockSpec((tm, d), lambda m: (m, 0))
    return pl.pallas_call(
        _proj_res_kernel,
        out_shape=(jax.ShapeDtypeStruct((s, d), F32), jax.ShapeDtypeStruct((s, d), BF16)),
        grid=(s // tm,),
        in_specs=[pl.BlockSpec((tm, kd), lambda m: (m, 0)),
                  pl.BlockSpec((None, kd, d), lambda m: (layer, 0, 0), pipeline_mode=pl.Buffered(1)),
                  row_spec,
                  pl.BlockSpec((1, d), lambda m: (0, 0))],
        out_specs=(row_spec, row_spec),
        compiler_params=_params("arbitrary"),
        name="proj_residual",
    )(y, w, h, norm_w.reshape(1, d))


def _mlp_kernel(u_ref, w1_ref, w2_ref, h_ref, *rest, emit_norm):
    if emit_norm:
        nw_ref, ho_ref, uo_ref = rest
    else:
        (ho_ref,) = rest
    j = pl.program_id(1)

    @pl.when(j == 0)
    def _():
        ho_ref[...] = jnp.zeros_like(ho_ref)

    a = jnp.dot(u_ref[...], w1_ref[...], preferred_element_type=F32)
    a = jnp.square(jnp.maximum(a, 0.0)).astype(BF16)
    ho_ref[...] += jnp.dot(a, w2_ref[...], preferred_element_type=F32)
    slab = h_ref.shape[0]
    rows = pl.ds(pl.multiple_of(j * slab, slab), slab)
    ho_ref[rows, :] += h_ref[...]

    if emit_norm:
        @pl.when(j == pl.num_programs(1) - 1)
        def _():
            uo_ref[...] = _rms_rows(ho_ref[...], nw_ref[...]).astype(uo_ref.dtype)


def _mlp(u, w1, w2, layer, h, norm_w, tm, th):
    s, d = u.shape
    hd = w1.shape[2]
    emit_norm = norm_w is not None
    n_hidden = hd // th
    slab = tm // n_hidden
    assert slab * n_hidden == tm and slab % SUBLANES == 0
    row_spec = pl.BlockSpec((tm, d), lambda m, j: (m, 0))
    in_specs = [row_spec,
                pl.BlockSpec((None, d, th), lambda m, j: (layer, 0, j)),
                pl.BlockSpec((None, th, d), lambda m, j: (layer, j, 0)),
                pl.BlockSpec((slab, d), lambda m, j: (m * n_hidden + j, 0))]
    args = [u, w1, w2, h]
    out_shape = [jax.ShapeDtypeStruct((s, d), F32)]
    out_specs = [row_spec]
    if emit_norm:
        in_specs.append(pl.BlockSpec((1, d), lambda m, j: (0, 0)))
        args.append(norm_w.reshape(1, d))
        out_shape.append(jax.ShapeDtypeStruct((s, d), BF16))
        out_specs.append(row_spec)
    out = pl.pallas_call(
        functools.partial(_mlp_kernel, emit_norm=emit_norm),
        out_shape=tuple(out_shape),
        grid=(s // tm, hd // th),
        in_specs=in_specs,
        out_specs=tuple(out_specs),
        compiler_params=_params("arbitrary", "arbitrary"),
        name="mlp_residual",
    )(*args)
    return out if emit_norm else (out[0], None)


def _att_in_kernel(u_ref, w_ref, nw_ref, o_ref, *wb_ref, n_qk_tiles):
    n = pl.program_id(0)
    wb_ref = _bf16_weight_tile(w_ref, *wb_ref)
    acc = jnp.dot(u_ref[...], wb_ref[...], preferred_element_type=F32)

    @pl.when(n >= n_qk_tiles)
    def _():
        o_ref[...] = acc.astype(o_ref.dtype)

    @pl.when(n < n_qk_tiles)
    def _():
        gi = lax.broadcasted_iota(jnp.int32, (NORM_SLAB, NORM_SLAB), 0) // ATT_QK_DIM
        gj = lax.broadcasted_iota(jnp.int32, (NORM_SLAB, NORM_SLAB), 1) // ATT_QK_DIM
        ones_bd = (gi == gj).astype(BF16)
        nw = nw_ref[...]
        for sl in range(acc.shape[1] // NORM_SLAB):
            a = acc[:, sl * NORM_SLAB:(sl + 1) * NORM_SLAB]
            ss = jnp.dot((a * a).astype(BF16), ones_bd, preferred_element_type=F32)
            out = a * lax.rsqrt(ss * (1.0 / ATT_QK_DIM) + EPS) * nw[:, sl * NORM_SLAB:(sl + 1) * NORM_SLAB]
            o_ref[:, sl * NORM_SLAB:(sl + 1) * NORM_SLAB] = out.astype(o_ref.dtype)


def _att_in_proj(u, w, layer, qk_w, n_qk, tm, tn):
    s, d = u.shape
    n_out = w.shape[2]
    n_qk_tiles = n_qk // tn
    kernel = functools.partial(_att_in_kernel, n_qk_tiles=n_qk_tiles)
    return pl.pallas_call(
        kernel,
        out_shape=jax.ShapeDtypeStruct((s, n_out), BF16),
        grid=(n_out // tn, s // tm),
        in_specs=[pl.BlockSpec((tm, d), lambda n, m: (m, 0)),
                  _weight_tile_spec(layer, d, tn, 0),
                  pl.BlockSpec((1, tn), lambda n, m: (0, jnp.minimum(n, n_qk_tiles - 1)))],
        out_specs=pl.BlockSpec((tm, tn), lambda n, m: (m, n)),
        scratch_shapes=_weight_scratch(w, d, tn),
        compiler_params=_params("arbitrary", "arbitrary"),
        name="att_in_proj",
    )(u, w, qk_w)


def _bias_kernel(rel_ref, far_ref, bk_ref, o_ref):
    h = pl.program_id(0)
    bk = bk_ref[...]
    far = far_ref[h]
    acc = jnp.zeros(bk.shape, F32)
    for b in range(NUM_BUCKETS):
        acc = jnp.where(bk == b, (rel_ref[b, h] - far) * LOG2E, acc)
    n_far = o_ref.shape[2] - bk.shape[1]
    o_ref[0, :, :n_far] = jnp.zeros((bk.shape[0], n_far), F32)
    o_ref[0, :, n_far:] = jnp.where(bk < 0, MASK_VALUE, acc)


def _bias_tiles(rel_bias, far_bias, buckets):
    nb, nh = rel_bias.shape
    t, band = buckets.shape
    return pl.pallas_call(
        _bias_kernel,
        out_shape=jax.ShapeDtypeStruct((nh, t, 2 * t), F32),
        grid=(nh,),
        in_specs=[pl.BlockSpec(memory_space=pltpu.SMEM),
                  pl.BlockSpec(memory_space=pltpu.SMEM),
                  pl.BlockSpec((t, band), lambda h: (0, 0))],
        out_specs=pl.BlockSpec((1, t, 2 * t), lambda h: (h, 0, 0)),
        compiler_params=_params("arbitrary"),
        name="att_bias_tiles",
    )(rel_bias, far_bias, buckets)


def _t5_bucket(rel):
    nb = NUM_BUCKETS // 2
    ret = (rel > 0).astype(jnp.int32) * nb
    n = jnp.abs(rel)
    max_exact = nb // 2
    nf = jnp.maximum(n, max_exact).astype(F32)
    large = max_exact + (jnp.log(nf / max_exact) / math.log(MAX_DISTANCE / max_exact)
                         * (nb - max_exact)).astype(jnp.int32)
    large = jnp.minimum(large, nb - 1)
    return ret + jnp.where(n < max_exact, n, large)


def _stack_maps(q):
    lo = lax.broadcasted_iota(jnp.int32, (1, LANES), 1) < ATT_QK_DIM
    zero16 = jnp.zeros((), BF16)
    return jnp.concatenate([jnp.where(lo, q, zero16), jnp.where(lo, zero16, q)], axis=0)


def _attn_finish(acc, l, lam, sw, out_scale, t, o_ref):
    o = acc / l
    o = o[:t] - lam * o[t:]
    o_ref[...] = (_rms_rows(o, sw) * out_scale).astype(o_ref.dtype)


def _attn_fixed_kernel(lam_ref, shift_ref, q_ref, k_ref, v_ref, bias_ref, sw_ref, o_ref, l_ref, acc_ref,
                       *, tile, far_widths, out_scale):
    shift = shift_ref[pl.program_id(0)]
    qi = pl.program_id(1)
    t = tile
    qq = _stack_maps(q_ref[...])
    l_ref[...] = jnp.zeros_like(l_ref)
    acc_ref[...] = jnp.zeros_like(acc_ref)

    def step(first_tile, n_tiles, bias):
        width = n_tiles * t
        start = pl.multiple_of(first_tile * t, t)
        k = k_ref[pl.ds(start, width), :]
        v = v_ref[pl.ds(start, width), :]
        s = lax.dot_general(qq, k, (((1,), (1,)), ((), ())), preferred_element_type=F32)
        if bias is None:
            p = jnp.exp2(s - shift)
        else:
            b = bias - shift
            p = jnp.exp2(s + jnp.concatenate([b, b], axis=0))
        part = p[:, :LANES]
        for c in range(1, width // LANES):
            part = part + p[:, c * LANES:(c + 1) * LANES]
        l_ref[...] += part
        acc_ref[...] += jnp.dot(p.astype(BF16), v, preferred_element_type=F32)

    n_far = jnp.maximum(qi - 1, 0)
    w0 = far_widths[0]

    def wide_body(i, carry):
        step(i * w0, w0, None)
        return carry

    lax.fori_loop(0, n_far // w0, wide_body, 0)
    for w in far_widths[1:]:
        @pl.when((n_far & w) != 0)
        def _(w=w):
            step(n_far & ~(2 * w - 1), w, None)

    @pl.when(qi > 0)
    def _():
        step(qi - 1, 2, bias_ref[0])

    @pl.when(qi == 0)
    def _():
        step(0, 1, bias_ref[0, :, t:])

    l = jnp.sum(l_ref[...], axis=-1, keepdims=True)
    _attn_finish(acc_ref[...], l, lam_ref[0], sw_ref[...], out_scale, t, o_ref)


def _attn_online_kernel(lam_ref, q_ref, k_ref, v_ref, bias_ref, sw_ref, o_ref, m_ref, l_ref, acc_ref,
                        *, tile, out_scale):
    qi = pl.program_id(1)
    t = tile
    qq = _stack_maps(q_ref[...])
    m_ref[...] = jnp.full_like(m_ref, MASK_VALUE)
    l_ref[...] = jnp.zeros_like(l_ref)
    acc_ref[...] = jnp.zeros_like(acc_ref)

    def step(kt, bias):
        start = pl.multiple_of(kt * t, t)
        k = k_ref[pl.ds(start, t), :]
        v = v_ref[pl.ds(start, t), :]
        s = lax.dot_general(qq, k, (((1,), (1,)), ((), ())), preferred_element_type=F32)
        if bias is not None:
            s = s + jnp.concatenate([bias, bias], axis=0)
        m_old = m_ref[...]
        m_new = jnp.maximum(m_old, jnp.max(s, axis=-1, keepdims=True))
        alpha = jnp.exp2(m_old - m_new)
        p = jnp.exp2(s - m_new)
        l_ref[...] = alpha * l_ref[...] + jnp.sum(p, axis=-1, keepdims=True)
        acc_ref[...] = alpha * acc_ref[...] + jnp.dot(p.astype(BF16), v, preferred_element_type=F32)
        m_ref[...] = m_new

    def far_body(kt, carry):
        step(kt, None)
        return carry

    lax.fori_loop(0, jnp.maximum(qi - 1, 0), far_body, 0)

    @pl.when(qi > 0)
    def _():
        step(qi - 1, bias_ref[0, :, :t])

    step(qi, bias_ref[0, :, t:])
    _attn_finish(acc_ref[...], l_ref[...], lam_ref[0], sw_ref[...], out_scale, t, o_ref)


def _attention(lam, shift, qkv, bias, subln_w, n_heads, tile, out_scale):
    s = qkv.shape[0]
    smem = pl.BlockSpec(memory_space=pltpu.SMEM)
    in_specs = [pl.BlockSpec((tile, LANES), lambda h, i: (i, h)),
                pl.BlockSpec((s, LANES), lambda h, i: (0, n_heads + h)),
                pl.BlockSpec((s, LANES), lambda h, i: (0, 2 * n_heads + h)),
                pl.BlockSpec((1, tile, 2 * tile), lambda h, i: (h, 0, 0)),
                pl.BlockSpec((1, ATT_V_DIM), lambda h, i: (0, 0))]
    args = (qkv, qkv, qkv, bias, subln_w.reshape(1, ATT_V_DIM))
    acc = pltpu.VMEM((2 * tile, ATT_V_DIM), F32)
    if shift is not None:
        kernel = functools.partial(_attn_fixed_kernel, tile=tile, far_widths=ATT_FAR_WIDTHS, out_scale=out_scale)
        in_specs = [smem, smem] + in_specs
        args = (lam, shift) + args
        scratch = [pltpu.VMEM((2 * tile, LANES), F32), acc]
        name = "diff_attention"
    else:
        kernel = functools.partial(_attn_online_kernel, tile=tile, out_scale=out_scale)
        in_specs = [smem] + in_specs
        args = (lam,) + args
        scratch = [pltpu.VMEM((2 * tile, 1), F32), pltpu.VMEM((2 * tile, 1), F32), acc]
        name = "diff_attention_online"
    return pl.pallas_call(
        kernel,
        out_shape=jax.ShapeDtypeStruct((s, n_heads * ATT_V_DIM), BF16),
        grid=(n_heads, s // tile),
        in_specs=in_specs,
        out_specs=pl.BlockSpec((tile, ATT_V_DIM), lambda h, i: (i, h)),
        scratch_shapes=scratch,
        compiler_params=_params("arbitrary", "arbitrary"),
        name=name,
    )(*args)


def kernel(x, norm_mix_w, norm_mlp_w, ssm_in_w, ssm_conv_w, ssm_conv_b, ssm_dt_bias, ssm_a_log, ssm_d, ssm_norm_w, ssm_out_w, att_in_w, att_q_norm_w, att_k_norm_w, att_lam_q1, att_lam_k1, att_lam_q2, att_lam_k2, att_subln_w, att_out_w, rel_bias, mlp_w1, mlp_w2):
    bsz, s, d = x.shape
    assert bsz == 1
    depth = norm_mix_w.shape[0]
    d_inner = ssm_out_w.shape[1]
    n_ssm_heads = ssm_dt_bias.shape[1]
    conv_dim = ssm_conv_w.shape[2]
    n_att_heads = rel_bias.shape[1]
    r = n_ssm_heads // SSM_GROUPS

    tm = min(ROW_TILE, s)
    tmr = min(RES_ROW_TILE, s)
    chunk = min(SSD_CHUNK, s)
    tile = min(ATT_TILE, s)
    assert s % tm == 0 and s % tmr == 0 and s % chunk == 0 and s % tile == 0
    assert tile % ATT_CHUNK == 0 and tile >= BIAS_BAND

    h = x.reshape(s, d)
    u = _rmsnorm(h, norm_mix_w[0], tmr)

    ti = jnp.arange(tile, dtype=jnp.int32)[:, None]
    tj = jnp.arange(tile + BIAS_BAND, dtype=jnp.int32)[None, :] - BIAS_BAND
    allowed = (tj < 0) | ((tj // ATT_CHUNK) <= (ti // ATT_CHUNK))
    buckets = jnp.where(allowed, _t5_bucket(tj - ti), -1)
    far_bias = rel_bias[_t5_bucket(jnp.int32(-2 * tile))]
    bias = _bias_tiles(rel_bias, far_bias, buckets)
    near_bias = rel_bias.astype(F32) - far_bias[None, :]
    near_hi = jnp.maximum(jnp.max(near_bias, axis=0), 0.0)
    near_lo = jnp.minimum(jnp.min(near_bias, axis=0), 0.0)

    ssm_in_wb = ssm_in_w.astype(BF16)
    ssm_out_wb, att_out_wb = ssm_out_w.astype(BF16), att_out_w.astype(BF16)
    mlp_w1b, mlp_w2b = mlp_w1.astype(BF16), mlp_w2.astype(BF16)

    for i in range(depth):
        j = i // 2
        if i % 2 == 0:
            w_dt = ssm_in_wb[j, :, d_inner + conv_dim:]
            z = _proj(u, ssm_in_wb, j, 0, d_inner, 0.5, tm, COL_TILE)
            xbc = _ssm_xbc_proj(u, ssm_in_wb, j, 0.5 * ssm_conv_w[j], 0.5 * ssm_conv_b[j].reshape(1, conv_dim),
                                d_inner, tm, XBC_COL_TILE)
            a = -jnp.exp(ssm_a_log[j].astype(F32))
            mult = jnp.concatenate([jnp.ones_like(a), a * LOG2E]).reshape(1, -1)
            dt_b = jnp.tile(ssm_dt_bias[j].astype(F32), 2).reshape(1, -1)
            nar = _dt_proj(u, jnp.concatenate([w_dt, w_dt], axis=1), dt_b, mult, chunk, tm)
            cols = nar.reshape(s, 2, SSM_GROUPS, r).transpose(2, 0, 1, 3).reshape(SSM_GROUPS, s, 2 * r)
            rows = cols.transpose(0, 2, 1)
            d_exp = jnp.repeat(ssm_d[j].astype(F32), SSM_HEAD_DIM).reshape(1, d_inner)
            y = _ssd(z, xbc, cols, rows, d_exp, ssm_norm_w[j].reshape(1, d_inner), d_inner, chunk)
            h, u = _proj_res(y, ssm_out_wb, j, h, norm_mlp_w[i], tmr)
        else:
            lambda_init = 0.8 - 0.6 * math.exp(-0.3 * i)
            lam = (jnp.exp(jnp.sum(att_lam_q1[j].astype(F32) * att_lam_k1[j].astype(F32)))
                   - jnp.exp(jnp.sum(att_lam_q2[j].astype(F32) * att_lam_k2[j].astype(F32)))
                   + lambda_init).reshape(1)
            reps = d // ATT_QK_DIM
            wq = att_q_norm_w[j].astype(F32)
            wk = att_k_norm_w[j].astype(F32)
            qk_w = jnp.concatenate([jnp.tile(wq, reps) * (ATT_QK_DIM ** -0.5 * LOG2E),
                                    jnp.tile(wk, reps)]).reshape(1, 2 * d)
            qkv = _att_in_proj(u, att_in_w, j, qk_w, 2 * d, tm, COL_TILE)
            logit_bound = (ATT_QK_DIM ** 0.5) * jnp.max(jnp.abs(wq * wk))
            shift = (logit_bound + near_hi) * LOG2E
            spread = jnp.max((2.0 * logit_bound + near_hi - near_lo) * LOG2E)
            att_args = (qkv, bias, att_subln_w[j], n_att_heads, tile, 1.0 - lambda_init)
            o = lax.cond(spread <= EXP2_SAFE_RANGE,
                         lambda: _attention(lam, shift, *att_args),
                         lambda: _attention(lam, None, *att_args))
            h, u = _proj_res(o, att_out_wb, j, h, norm_mlp_w[i], tmr)
        nw_next = norm_mix_w[i + 1] if i + 1 < depth else None
        h, u = _mlp(u, mlp_w1b, mlp_w2b, i, h, nw_next, tm, MLP_HIDDEN_TILE)
    return h.reshape(bsz, s, d)
```
